```python
import functools
import jax, jax.numpy as jnp
from jax import lax
import numpy as np

D_MODEL = 1024
BATCH = 2
SEQ = 8192
DEPTH = 2
DEC_BATCH = 32
DEC_SEQ = 1
PAST_LEN = 16384
PAGE_SIZE = 128

RET_HEADS = 4
RET_DK = D_MODEL // 8
RET_DV = D_MODEL // 4
RET_CHUNK = 128
RET_THETA = 10000.0
CONV_CH = D_MODEL
CONV_W = 31
ATT_HEADS = 8
KV_HEADS = 4
HEAD_DIM = D_MODEL // ATT_HEADS
IDX_HEADS = 8
IDX_DIM = 64
TOPK_MAX = 256
ATT_BLOCK = 128
ROPE_THETA = 500000.0
ROT_DIM = HEAD_DIM // 4
IDX_ROT = IDX_DIM // 4
FF_DIM = 2816
N_BRANCH = 3
EPS = 1e-6
IN_COLS = (2 * RET_HEADS * RET_DK + 2 * RET_HEADS * RET_DV + 2 * CONV_CH + ATT_HEADS * HEAD_DIM
           + 2 * KV_HEADS * HEAD_DIM + IDX_HEADS * IDX_DIM + IDX_DIM + IDX_HEADS + N_BRANCH * D_MODEL)

kernel_name = 'hybrid_retention_conformer_dsa_step'


def split_points():
    widths = [RET_HEADS * RET_DK, RET_HEADS * RET_DK, RET_HEADS * RET_DV, RET_HEADS * RET_DV,
              2 * CONV_CH, ATT_HEADS * HEAD_DIM, KV_HEADS * HEAD_DIM, KV_HEADS * HEAD_DIM,
              IDX_HEADS * IDX_DIM, IDX_DIM, IDX_HEADS, N_BRANCH * D_MODEL]
    return [int(v) for v in np.cumsum(widths)[:-1]]


def rmsnorm(x, g):
    xf = x.astype(jnp.float32)
    y = xf * lax.rsqrt(jnp.mean(xf * xf, axis=-1, keepdims=True) + EPS)
    return (y * g.astype(jnp.float32)).astype(x.dtype)


def rms_unit(x):
    xf = x.astype(jnp.float32)
    return (xf * lax.rsqrt(jnp.mean(xf * xf, axis=-1, keepdims=True) + EPS)).astype(x.dtype)


def rotary(x, pos, rot_dim, theta):
    half = rot_dim // 2
    inv = 1.0 / (theta ** (jnp.arange(half, dtype=jnp.float32) / half))
    ang = pos.astype(jnp.float32)[:, None] * inv[None, :]
    cos = jnp.cos(ang)[:, None, :]
    sin = jnp.sin(ang)[:, None, :]
    x1 = x[..., :half].astype(jnp.float32)
    x2 = x[..., half:rot_dim].astype(jnp.float32)
    rot = jnp.concatenate([x1 * cos - x2 * sin, x2 * cos + x1 * sin], axis=-1).astype(x.dtype)
    return jnp.concatenate([rot, x[..., rot_dim:]], axis=-1)


def modulate(h, shift, scale):
    return h * (1.0 + scale) + shift


def swiglu(h, w_up, w_down):
    a, b = jnp.split(h @ w_up, 2, axis=-1)
    return (jax.nn.silu(a) * b) @ w_down


def retention(q, k, v, s0):
    B, T, H, DK = q.shape
    DV = v.shape[-1]
    C = RET_CHUNK if T % RET_CHUNK == 0 else T
    N = T // C
    log_g = jnp.log(1.0 - 2.0 ** (-5.0 - jnp.arange(H, dtype=jnp.float32)))
    idx = jnp.arange(C, dtype=jnp.float32)
    diff = idx[:, None] - idx[None, :]
    dmask = jnp.where(diff >= 0, jnp.exp(jnp.maximum(diff, 0.0)[None] * log_g[:, None, None]), 0.0)
    cross = jnp.exp((idx + 1.0)[None, :] * log_g[:, None])
    kdec = jnp.exp((C - 1.0 - idx)[None, :] * log_g[:, None])
    g_c = jnp.exp(C * log_g)

    def to_chunks(a):
        return a.reshape(B, N, C, H, a.shape[-1]).transpose(1, 0, 3, 2, 4).astype(jnp.float32)

    def step(S, inp):
        qi, ki, vi = inp
        sc = jnp.einsum('bhid,bhjd->bhij', qi, ki) * dmask[None]
        inner = jnp.einsum('bhij,bhjv->bhiv', sc, vi)
        crs = jnp.einsum('bhid,bhdv->bhiv', qi, S) * cross[None, :, :, None]
        S_new = g_c[None, :, None, None] * S + jnp.einsum('bhjd,bhjv->bhdv', ki * kdec[None, :, :, None], vi)
        return S_new, inner + crs

    S_fin, out = lax.scan(step, s0.astype(jnp.float32), (to_chunks(q), to_chunks(k), to_chunks(v)))
    out = out.transpose(1, 0, 3, 2, 4).reshape(B, T, H, DV)
    return out.astype(v.dtype), S_fin.astype(s0.dtype)


def causal_dwconv(u, buf, w, b):
    full = jnp.concatenate([buf.astype(u.dtype), u], axis=1)
    y = lax.conv_general_dilated(full, w[:, None, :].astype(u.dtype), window_strides=(1,), padding='VALID',
                                 dimension_numbers=('NWC', 'WIO', 'NWC'), feature_group_count=u.shape[-1])
    return y + b.astype(u.dtype), full[:, -(CONV_W - 1):]


def index_scores(iq, iw, ik):
    dots = jax.nn.relu(jnp.einsum('bthd,bsd->bths', iq, ik).astype(jnp.float32))
    return jnp.einsum('bths,bth->bts', dots, iw.astype(jnp.float32))


def sparse_attend(q, ksel, vsel, valid):
    B, Tq, H, Dh = q.shape
    G = H // KV_HEADS
    qg = q.reshape(B, Tq, KV_HEADS, G, Dh)
    s = jnp.einsum('btkgd,btskd->btkgs', qg, ksel).astype(jnp.float32) * (Dh ** -0.5)
    s = jnp.where(valid[:, :, None, None, :], s, -jnp.inf)
    p = jax.nn.softmax(s, axis=-1).astype(vsel.dtype)
    o = jnp.einsum('btkgs,btskd->btkgd', p, vsel)
    return o.reshape(B, Tq, H, Dh)


def prompt_sparse_attention(q, k, v, iq, ik, iw):
    B, T, H, Dh = q.shape
    topk = min(TOPK_MAX, T // 4)
    nblk = T // ATT_BLOCK

    def blocks(a):
        return a.reshape((B, nblk, ATT_BLOCK) + a.shape[2:]).swapaxes(0, 1)

    starts = jnp.arange(nblk, dtype=jnp.int32) * ATT_BLOCK
    spos = jnp.arange(T, dtype=jnp.int32)

    def block(args):
        qi, iqi, iwi, t0 = args
        tpos = t0 + jnp.arange(ATT_BLOCK, dtype=jnp.int32)
        valid = spos[None, :] <= tpos[:, None]
        sc = jnp.where(valid[None], index_scores(iqi, iwi, ik), -jnp.inf)
        _, sel = lax.top_k(sc, topk)
        sel_valid = sel <= tpos[None, :, None]
        ksel = jax.vmap(lambda kb, ib: kb[ib])(k, sel)
        vsel = jax.vmap(lambda vb, ib: vb[ib])(v, sel)
        return sparse_attend(qi, ksel, vsel, sel_valid)

    out = lax.map(block, (blocks(q), blocks(iq), blocks(iw), starts))
    return out.swapaxes(0, 1).reshape(B, T, H, Dh)


def sample_sparse_attention(q, k, v, iq, ik, iw, cache_k_l, cache_v_l, cache_ik_l, page_table):
    B, T, H, Dh = q.shape
    past = page_table.shape[1] * PAGE_SIZE
    L = past + T
    topk = min(TOPK_MAX, L // 4)
    ik_past = cache_ik_l[page_table].reshape(B, past, IDX_DIM).astype(ik.dtype)
    ik_all = jnp.concatenate([ik_past, ik], axis=1)
    tpos = past + jnp.arange(T, dtype=jnp.int32)
    spos = jnp.arange(L, dtype=jnp.int32)
    valid = spos[None, :] <= tpos[:, None]
    sc = jnp.where(valid[None], index_scores(iq, iw, ik_all), -jnp.inf)
    _, sel = lax.top_k(sc, topk)
    sel_valid = sel <= tpos[None, :, None]
    in_past = (sel < past)[..., None, None]
    ps = jnp.minimum(sel, past - 1)
    phys = jax.vmap(lambda pt, i: pt[i])(page_table, ps // PAGE_SIZE)
    slot = ps % PAGE_SIZE
    new_i = jnp.clip(sel - past, 0, T - 1)
    k_new = jax.vmap(lambda kb, ib: kb[ib])(k, new_i)
    v_new = jax.vmap(lambda vb, ib: vb[ib])(v, new_i)
    ksel = jnp.where(in_past, cache_k_l[phys, slot].astype(k.dtype), k_new)
    vsel = jnp.where(in_past, cache_v_l[phys, slot].astype(v.dtype), v_new)
    return sparse_attend(q, ksel, vsel, sel_valid)


def mixer(h, pos, ret_s0, conv_buf, attn_fn, w_in, conv_w, conv_b, conv_g, q_g, k_g, ik_g,
          w_ret_o, w_conv_o, w_att_o, w_o):
    B, T, _ = h.shape
    z = h @ w_in
    rq, rk, rv, rg, cu, aq, ak, av, iq, ik, iw, gl = jnp.split(z, split_points(), axis=-1)
    rq = rotary(rq.reshape(B, T, RET_HEADS, RET_DK), pos, RET_DK, RET_THETA)
    rk = rotary(rk.reshape(B, T, RET_HEADS, RET_DK), pos, RET_DK, RET_THETA) * (RET_DK ** -0.5)
    ro, ret_state = retention(rq, rk, rv.reshape(B, T, RET_HEADS, RET_DV), ret_s0)
    ro = rms_unit(ro).reshape(B, T, RET_HEADS * RET_DV)
    ret_out = (jax.nn.silu(rg) * ro) @ w_ret_o
    u = cu[..., :CONV_CH] * jax.nn.sigmoid(cu[..., CONV_CH:])
    cy, conv_state = causal_dwconv(u, conv_buf, conv_w, conv_b)
    conv_out = jax.nn.silu(rmsnorm(cy, conv_g)) @ w_conv_o
    aq = rotary(rmsnorm(aq.reshape(B, T, ATT_HEADS, HEAD_DIM), q_g), pos, ROT_DIM, ROPE_THETA)
    ak = rotary(rmsnorm(ak.reshape(B, T, KV_HEADS, HEAD_DIM), k_g), pos, ROT_DIM, ROPE_THETA)
    av = av.reshape(B, T, KV_HEADS, HEAD_DIM)
    iq = rotary(iq.reshape(B, T, IDX_HEADS, IDX_DIM), pos, IDX_ROT, ROPE_THETA)
    ik = rotary(rmsnorm(ik, ik_g)[:, :, None, :], pos, IDX_ROT, ROPE_THETA)[:, :, 0]
    iw = iw * (IDX_HEADS ** -0.5 * IDX_DIM ** -0.5)
    ao = attn_fn(aq, ak, av, iq, ik, iw).reshape(B, T, ATT_HEADS * HEAD_DIM)
    att_out = ao @ w_att_o
    g = jax.nn.sigmoid(gl).reshape(B, T, N_BRANCH, D_MODEL)
    merged = g[:, :, 0] * ret_out + g[:, :, 1] * conv_out + g[:, :, 2] * att_out
    return merged @ w_o, (ak, av, ik, ret_state, conv_state)


def decoder_layer(x, c, pos, ret_s0, conv_buf, attn_fn, lw):
    (w_ada, b_ada, norm_g, w_ff1_up, w_ff1_down, w_ff2_up, w_ff2_down, w_in, conv_w, conv_b, conv_g,
     q_g, k_g, ik_g, w_ret_o, w_conv_o, w_att_o, w_o) = lw
    mod = jax.nn.silu(c) @ w_ada + b_ada
    sh1, sc1, g1, sh2, sc2, g2, sh3, sc3, g3 = jnp.split(mod[:, None, :], 9, axis=-1)
    x = x + 0.5 * g1 * swiglu(modulate(rmsnorm(x, norm_g[0]), sh1, sc1), w_ff1_up, w_ff1_down)
    m, st = mixer(modulate(rmsnorm(x, norm_g[1]), sh2, sc2), pos, ret_s0, conv_buf, attn_fn, w_in,
                  conv_w, conv_b, conv_g, q_g, k_g, ik_g, w_ret_o, w_conv_o, w_att_o, w_o)
    x = x + g2 * m
    x = x + 0.5 * g3 * swiglu(modulate(rmsnorm(x, norm_g[2]), sh3, sc3), w_ff2_up, w_ff2_down)
    return x, st


def setup_inputs(seed: int = 0) -> dict:
    key = jax.random.key(seed)
    ks = jax.random.split(key, 32)
    D = D_MODEL
    n_pages = PAST_LEN // PAGE_SIZE
    n_used = DEC_BATCH * n_pages
    n_pool = (5 * n_used) // 4

    def nrm(k, shape, s):
        return jax.random.normal(k, shape, jnp.float32) * s

    page_table = jax.random.permutation(ks[7], n_pool)[:n_used].reshape(DEC_BATCH, n_pages).astype(jnp.int32)
    return {
        'x_prompt': nrm(ks[0], (BATCH, SEQ, D), 1.0),
        'x_sample': nrm(ks[1], (DEC_BATCH, DEC_SEQ, D), 1.0),
        'cache_k': nrm(ks[2], (DEPTH, n_pool, PAGE_SIZE, KV_HEADS, HEAD_DIM), 1.0),
        'cache_v': nrm(ks[3], (DEPTH, n_pool, PAGE_SIZE, KV_HEADS, HEAD_DIM), 1.0),
        'cache_idx_k': nrm(ks[4], (DEPTH, n_pool, PAGE_SIZE, IDX_DIM), 1.0),
        'state_ret': nrm(ks[5], (DEPTH, DEC_BATCH, RET_HEADS, RET_DK, RET_DV), 0.1),
        'state_conv': nrm(ks[6], (DEPTH, DEC_BATCH, CONV_W - 1, CONV_CH), 0.5),
        'page_table': page_table,
        'c_prompt': nrm(ks[8], (BATCH, D), 1.0),
        'c_sample': nrm(ks[9], (DEC_BATCH, D), 1.0),
        'w_ada': nrm(ks[10], (DEPTH, D, 9 * D), 0.5 * D ** -0.5),
        'b_ada': nrm(ks[11], (DEPTH, 9 * D), 0.02),
        'norm_g': 1.0 + nrm(ks[12], (DEPTH, 3, D), 0.05),
        'w_ff1_up': nrm(ks[13], (DEPTH, D, 2 * FF_DIM), D ** -0.5),
        'w_ff1_down': nrm(ks[14], (DEPTH, FF_DIM, D), FF_DIM ** -0.5),
        'w_ff2_up': nrm(ks[15], (DEPTH, D, 2 * FF_DIM), D ** -0.5),
        'w_ff2_down': nrm(ks[16], (DEPTH, FF_DIM, D), FF_DIM ** -0.5),
        'w_in': nrm(ks[17], (DEPTH, D, IN_COLS), D ** -0.5),
        'conv_w': nrm(ks[18], (DEPTH, CONV_W, CONV_CH), CONV_W ** -0.5),
        'conv_b': nrm(ks[19], (DEPTH, CONV_CH), 0.02),
        'conv_g': 1.0 + nrm(ks[20], (DEPTH, CONV_CH), 0.05),
        'q_norm_g': 1.0 + nrm(ks[21], (DEPTH, HEAD_DIM), 0.05),
        'k_norm_g': 1.0 + nrm(ks[22], (DEPTH, HEAD_DIM), 0.05),
        'idx_k_norm_g': 1.0 + nrm(ks[23], (DEPTH, IDX_DIM), 0.05),
        'w_ret_o': nrm(ks[24], (DEPTH, RET_HEADS * RET_DV, D), (RET_HEADS * RET_DV) ** -0.5),
        'w_conv_o': nrm(ks[25], (DEPTH, CONV_CH, D), CONV_CH ** -0.5),
        'w_att_o': nrm(ks[26], (DEPTH, ATT_HEADS * HEAD_DIM, D), (ATT_HEADS * HEAD_DIM) ** -0.5),
        'w_o': nrm(ks[27], (DEPTH, D, D), D ** -0.5),
    }


def reference(x_prompt, x_sample, cache_k, cache_v, cache_idx_k, state_ret, state_conv, page_table,
              c_prompt, c_sample, w_ada, b_ada, norm_g, w_ff1_up, w_ff1_down, w_ff2_up, w_ff2_down,
              w_in, conv_w, conv_b, conv_g, q_norm_g, k_norm_g, idx_k_norm_g, w_ret_o, w_conv_o,
              w_att_o, w_o):
    B, T_p, _ = x_prompt.shape
    T_s = x_sample.shape[1]
    pos_p = jnp.arange(T_p, dtype=jnp.int32)
    pos_s = page_table.shape[1] * PAGE_SIZE + jnp.arange(T_s, dtype=jnp.int32)
    xp, xs = x_prompt, x_sample
    sts_p, sts_s = [], []
    for l in range(DEPTH):
        lw = (w_ada[l], b_ada[l], norm_g[l], w_ff1_up[l], w_ff1_down[l], w_ff2_up[l], w_ff2_down[l],
              w_in[l], conv_w[l], conv_b[l], conv_g[l], q_norm_g[l], k_norm_g[l], idx_k_norm_g[l],
              w_ret_o[l], w_conv_o[l], w_att_o[l], w_o[l])
        ret0 = jnp.zeros((B, RET_HEADS, RET_DK, RET_DV), jnp.float32)
        buf0 = jnp.zeros((B, CONV_W - 1, CONV_CH), xp.dtype)
        xp, st_p = decoder_layer(xp, c_prompt, pos_p, ret0, buf0, prompt_sparse_attention, lw)
        attn_s = functools.partial(sample_sparse_attention, cache_k_l=cache_k[l], cache_v_l=cache_v[l],
                                   cache_ik_l=cache_idx_k[l], page_table=page_table)
        xs, st_s = decoder_layer(xs, c_sample, pos_s, state_ret[l], state_conv[l], attn_s, lw)
        sts_p.append(st_p)
        sts_s.append(st_s)
    k_prompt = jnp.stack([s[0] for s in sts_p])
    v_prompt = jnp.stack([s[1] for s in sts_p])
    idxk_prompt = jnp.stack([s[2] for s in sts_p])
    ret_prompt = jnp.stack([s[3] for s in sts_p])
    conv_prompt = jnp.stack([s[4] for s in sts_p])
    k_sample = jnp.stack([s[0] for s in sts_s])
    v_sample = jnp.stack([s[1] for s in sts_s])
    idxk_sample = jnp.stack([s[2] for s in sts_s])
    ret_sample = jnp.stack([s[3] for s in sts_s])
    conv_sample = jnp.stack([s[4] for s in sts_s])
    return (xp, xs, k_prompt, v_prompt, idxk_prompt, ret_prompt, conv_prompt,
            k_sample, v_sample, idxk_sample, ret_sample, conv_sample)
```

```python
import functools

import numpy as np
import jax
import jax.numpy as jnp
from jax import lax
from jax.experimental import pallas as pl
from jax.experimental.pallas import tpu as pltpu

F32 = jnp.float32
BF16 = jnp.bfloat16
I32 = jnp.int32

EPS = 1e-6
LANES = 128
NEG = -1e30
INT_MIN = -2 ** 31

RET_HEADS = 4
RET_DK = 128
RET_DV = 256
RET_CHUNK = 128
RET_THETA = 10000.0
CONV_W = 31
ATT_HEADS = 8
KV_HEADS = 4
HEAD_DIM = 128
IDX_HEADS = 8
IDX_DIM = 64
TOPK_MAX = 256
ROPE_THETA = 500000.0
ROT_DIM = HEAD_DIM // 4
IDX_ROT = IDX_DIM // 4
PAGE = 128

Z_RQ, Z_RK, Z_RV, Z_RG = 0, 512, 1024, 2048
Z_CU, Z_CG, Z_AQ, Z_GL = 3072, 4096, 5120, 6144
Z_AK, Z_AV, Z_IQ, Z_IX, Z_END = 9216, 9728, 10240, 10752, 10880

VMEM_LIMIT = 56 * 1024 * 1024


def _cp(*sem):
    return pltpu.CompilerParams(dimension_semantics=tuple(sem), vmem_limit_bytes=VMEM_LIMIT)


def _dot(a, b):
    return jnp.dot(a, b, preferred_element_type=F32)


def _dot_nt(a, b):
    return lax.dot_general(a, b, (((1,), (1,)), ((), ())), preferred_element_type=F32)


def _split(x):
    hi = x.astype(BF16)
    lo = (x - hi.astype(F32)).astype(BF16)
    return hi, lo


def _dot3(a, b, nt=False):
    d = _dot_nt if nt else _dot
    ah, al = _split(a)
    bh, bl = _split(b)
    return d(ah, bh) + d(ah, bl) + d(al, bh)


def _silu(x):
    return x * jax.nn.sigmoid(x)


def _rms(x):
    return x * lax.rsqrt(jnp.mean(x * x, axis=-1, keepdims=True) + EPS)


def _norm_mod(x, g, shift, scale):
    return _rms(x) * g * (1.0 + scale) + shift


def _rot(x, cos, s_lo, s_hi, half):
    return x * cos + pltpu.roll(x, LANES - half, 1) * s_lo + pltpu.roll(x, half, 1) * s_hi


def _sortable(s):
    s = jnp.where(s == 0.0, 0.0, s)
    bits = pltpu.bitcast(s, I32)
    return bits ^ ((bits >> 31) & 0x7FFFFFFF)


def _ada_kernel(c_ref, w_ref, b_ref, o_ref):
    o_ref[...] = _dot3(_silu(c_ref[...]), w_ref[...]) + b_ref[...]


def _ada_mod(c, w_ada, b_ada, layer):
    R, D = c.shape
    N = w_ada.shape[2]
    tn = 1024
    return pl.pallas_call(
        _ada_kernel,
        out_shape=jax.ShapeDtypeStruct((R, N), F32),
        grid=(N // tn,),
        in_specs=[pl.BlockSpec((R, D), lambda j: (0, 0)),
                  pl.BlockSpec((None, D, tn), lambda j: (layer, 0, j)),
                  pl.BlockSpec((None, 1, tn), lambda j: (layer, 0, j))],
        out_specs=pl.BlockSpec((R, tn), lambda j: (0, j)),
        compiler_params=_cp("parallel"),
        name="ada_mod",
    )(c, w_ada, b_ada.reshape(b_ada.shape[0], 1, N))


def _ffn_kernel(x_ref, sh_ref, sc_ref, g_ref, ng_ref, wu_ref, wd_ref, o_ref, acc_ref, *, ff, tf):
    x = x_ref[...]
    h = _norm_mod(x, ng_ref[...], sh_ref[0], sc_ref[0]).astype(BF16)
    for j in range(ff // tf):
        a = _dot(h, wu_ref[:, j * tf:(j + 1) * tf])
        b = _dot(h, wu_ref[:, ff + j * tf:ff + (j + 1) * tf])
        act = (_silu(a) * b).astype(BF16)
        upd = _dot(act, wd_ref[j * tf:(j + 1) * tf, :])
        if j == 0:
            acc_ref[...] = upd
        else:
            acc_ref[...] += upd
    o_ref[...] = x + 0.5 * g_ref[0] * acc_ref[...]


def _mod_spec(mod, tm, rpg):
    rm = mod.shape[1]
    return pl.BlockSpec((1, rm, mod.shape[2]), lambda i: ((i * tm) // rpg, 0, 0))


def _ffn(x, shift, scale, gate, ng, w_up, w_down, tm, rpg):
    M, D = x.shape
    ff = w_down.shape[0]
    const = lambda i: (0, 0)
    return pl.pallas_call(
        functools.partial(_ffn_kernel, ff=ff, tf=256),
        out_shape=jax.ShapeDtypeStruct((M, D), F32),
        grid=(M // tm,),
        in_specs=[pl.BlockSpec((tm, D), lambda i: (i, 0)),
                  _mod_spec(shift, tm, rpg), _mod_spec(scale, tm, rpg), _mod_spec(gate, tm, rpg),
                  pl.BlockSpec((1, D), const),
                  pl.BlockSpec((D, 2 * ff), const),
                  pl.BlockSpec((ff, D), const)],
        out_specs=pl.BlockSpec((tm, D), lambda i: (i, 0)),
        scratch_shapes=[pltpu.VMEM((tm, D), F32)],
        compiler_params=_cp("parallel"),
        name="ffn",
    )(x, shift, scale, gate, ng, w_up, w_down)


def _inproj_kernel(x_ref, sh_ref, sc_ref, ng_ref, w_ref, o_ref, h_ref):
    @pl.when(pl.program_id(1) == 0)
    def _():
        h_ref[...] = _norm_mod(x_ref[...], ng_ref[...], sh_ref[0], sc_ref[0]).astype(BF16)

    o_ref[...] = _dot(h_ref[...], w_ref[...])


def _inproj(x, shift, scale, ng, w_cat, tm, rpg):
    M, D = x.shape
    N = w_cat.shape[1]
    tn = N // 5
    ms = lambda mod: pl.BlockSpec((1, mod.shape[1], D), lambda i, j: ((i * tm) // rpg, 0, 0))
    return pl.pallas_call(
        _inproj_kernel,
        out_shape=jax.ShapeDtypeStruct((M, N), F32),
        grid=(M // tm, N // tn),
        in_specs=[pl.BlockSpec((tm, D), lambda i, j: (i, 0)),
                  ms(shift), ms(scale),
                  pl.BlockSpec((1, D), lambda i, j: (0, 0)),
                  pl.BlockSpec((D, tn), lambda i, j: (0, j))],
        out_specs=pl.BlockSpec((tm, tn), lambda i, j: (i, j)),
        scratch_shapes=[pltpu.VMEM((tm, D), BF16)],
        compiler_params=_cp("parallel", "arbitrary"),
        name="inproj",
    )(x, shift, scale, ng, w_cat)


def _prep_kernel(aq_ref, ak_ref, av_ref, iq_ref, ix_ref,
                 ca_ref, sa1_ref, sa2_ref, ci_ref, si1_ref, si2_ref, qg_ref, kg_ref, ikg_ref,
                 qa_ref, k_ref, kb_ref, v_ref, iqr_ref, ixo_ref, idxk_ref, *t_refs, tck):
    ca, sa1, sa2 = ca_ref[...], sa1_ref[...], sa2_ref[...]
    ci, si1, si2 = ci_ref[...], si1_ref[...], si2_ref[...]
    qg, kg = qg_ref[...], kg_ref[...]
    q_scale = HEAD_DIM ** -0.5
    for h in range(ATT_HEADS):
        sl = slice(h * HEAD_DIM, (h + 1) * HEAD_DIM)
        y = _rot(_rms(aq_ref[:, sl]) * qg, ca, sa1, sa2, ROT_DIM // 2)
        qa_ref[:, sl] = (y * q_scale).astype(BF16)
    for h in range(KV_HEADS):
        sl = slice(h * HEAD_DIM, (h + 1) * HEAD_DIM)
        y = _rot(_rms(ak_ref[:, sl]) * kg, ca, sa1, sa2, ROT_DIM // 2)
        k_ref[:, sl] = y
        kb_ref[:, sl] = y.astype(BF16)
    av = av_ref[...]
    v_ref[...] = av
    for j in range(IDX_HEADS * IDX_DIM // LANES):
        sl = slice(j * LANES, (j + 1) * LANES)
        iqr_ref[:, sl] = _rot(iq_ref[:, sl], ci, si1, si2, IDX_ROT // 2)
    blk = ix_ref[...]
    lane = lax.broadcasted_iota(I32, blk.shape, 1)
    ikm = jnp.where(lane < IDX_DIM, blk, 0.0)
    ms = jnp.sum(ikm * ikm, axis=-1, keepdims=True) * (1.0 / IDX_DIM)
    ikr = _rot(ikm * lax.rsqrt(ms + EPS) * ikg_ref[...], ci, si1, si2, IDX_ROT // 2)
    iw_scale = IDX_HEADS ** -0.5 * IDX_DIM ** -0.5
    iws = jnp.where((lane >= IDX_DIM) & (lane < IDX_DIM + IDX_HEADS), blk * iw_scale, 0.0)
    ixo = ikr + iws
    ixo_ref[...] = ixo
    idxk_ref[...] = ikr[:, :IDX_DIM]
    if t_refs:
        vt_ref, iqth_ref, iqtl_ref, ik3_ref, iwt_ref = t_refs
        tm = av.shape[0]
        avt = av.T
        for c in range(tm // tck):
            vt_ref[c] = avt[:, c * tck:(c + 1) * tck].astype(BF16)
        iqt = iqr_ref[...].T
        hi = iqt.astype(BF16)
        iqth_ref[...] = hi
        iqtl_ref[...] = (iqt - hi.astype(F32)).astype(BF16)
        khi = ikr.astype(BF16).astype(F32)
        klo = ikr - khi
        ik3_ref[...] = jnp.concatenate([khi + pltpu.roll(klo, IDX_DIM, 1), khi], axis=1).astype(BF16)
        iwt_ref[...] = ixo.T[IDX_DIM:IDX_DIM + IDX_HEADS, :]


def _prep(z, tabs_a, tabs_i, qg, kg, ikg, tm, transposed, tck):
    M = z.shape[0]
    nt = M // tm
    tpb = tabs_a[0].shape[0] // tm
    tab = pl.BlockSpec((tm, LANES), lambda i: (i % tpb, 0))
    col = lambda w, off: pl.BlockSpec((tm, w), lambda i, o=off // w: (i, o))
    row = lambda w: pl.BlockSpec((tm, w), lambda i: (i, 0))
    vec = pl.BlockSpec((1, LANES), lambda i: (0, 0))
    in_specs = [col(1024, Z_AQ), col(512, Z_AK), col(512, Z_AV), col(512, Z_IQ), col(128, Z_IX)]
    in_specs += [tab] * 6 + [vec] * 3
    out_shape = [jax.ShapeDtypeStruct((M, 1024), BF16), jax.ShapeDtypeStruct((M, 512), F32),
                 jax.ShapeDtypeStruct((M, 512), BF16), jax.ShapeDtypeStruct((M, 512), F32),
                 jax.ShapeDtypeStruct((M, 512), F32), jax.ShapeDtypeStruct((M, LANES), F32),
                 jax.ShapeDtypeStruct((M, IDX_DIM), F32)]
    out_specs = [row(1024), row(512), row(512), row(512), row(512), row(LANES), row(IDX_DIM)]
    if transposed:
        out_shape += [jax.ShapeDtypeStruct((M // tck, 512, tck), BF16),
                      jax.ShapeDtypeStruct((512, M), BF16), jax.ShapeDtypeStruct((512, M), BF16),
                      jax.ShapeDtypeStruct((M, 256), BF16), jax.ShapeDtypeStruct((IDX_HEADS, M), F32)]
        out_specs += [pl.BlockSpec((tm // tck, 512, tck), lambda i: (i, 0, 0)),
                      pl.BlockSpec((512, tm), lambda i: (0, i)), pl.BlockSpec((512, tm), lambda i: (0, i)),
                      row(256), pl.BlockSpec((IDX_HEADS, tm), lambda i: (0, i))]
    return pl.pallas_call(
        functools.partial(_prep_kernel, tck=tck),
        out_shape=out_shape, grid=(nt,), in_specs=in_specs, out_specs=out_specs,
        compiler_params=_cp("parallel"), name="dsa_prep",
    )(z, z, z, z, z, *tabs_a, *tabs_i, qg, kg, ikg)


def _ret_kernel(q_ref, k_ref, v_ref, rg_ref, cos_ref, sin_ref, lg_ref, o_ref, st_ref, s_ref, *, nb):
    n = pl.program_id(0)
    C = RET_CHUNK

    @pl.when(n == 0)
    def _():
        s_ref[...] = jnp.zeros_like(s_ref)

    cos, sin = cos_ref[...], sin_ref[...]
    ri = lax.broadcasted_iota(I32, (C, C), 0).astype(F32)
    ci = lax.broadcasted_iota(I32, (C, C), 1).astype(F32)
    diff = ri - ci
    rv = lax.broadcasted_iota(I32, (C, RET_DV), 0).astype(F32)
    for h in range(RET_HEADS):
        lg = lg_ref[h:h + 1, :]
        lg2 = jnp.concatenate([lg, lg], axis=1)
        dmask = jnp.where(diff >= 0, jnp.exp(jnp.maximum(diff, 0.0) * lg), 0.0)
        cross = jnp.exp((rv + 1.0) * lg2)
        kdec = jnp.exp((C - 1.0 - ri) * lg)
        g_c = jnp.exp(C * lg2)
        for b in range(nb):
            qs = slice(h * RET_DK, (h + 1) * RET_DK)
            vs = slice(h * RET_DV, (h + 1) * RET_DV)
            q = q_ref[b, :, qs]
            k = k_ref[b, :, qs]
            v = v_ref[b, :, vs]
            qr = q * cos + pltpu.roll(q, RET_DK // 2, 1) * sin
            kr = (k * cos + pltpu.roll(k, RET_DK // 2, 1) * sin) * (RET_DK ** -0.5)
            s_old = s_ref[b * RET_HEADS + h]
            sc = _dot3(qr, kr, nt=True) * dmask
            out = _dot3(sc, v) + _dot3(qr, s_old) * cross
            s_ref[b * RET_HEADS + h] = g_c * s_old + _dot3((kr * kdec).T, v)
            rg = rg_ref[b, :, vs]
            o_ref[b, :, vs] = (_silu(rg) * _rms(out)).astype(BF16)

    @pl.when(n == pl.num_programs(0) - 1)
    def _():
        for b in range(nb):
            for h in range(RET_HEADS):
                st_ref[b, h] = s_ref[b * RET_HEADS + h]


def _retention_prompt(z3, cos_r, sin_r, lg):
    B, T, _ = z3.shape
    C = RET_CHUNK
    return pl.pallas_call(
        functools.partial(_ret_kernel, nb=B),
        out_shape=[jax.ShapeDtypeStruct((B, T, 1024), BF16),
                   jax.ShapeDtypeStruct((B, RET_HEADS, RET_DK, RET_DV), F32)],
        grid=(T // C,),
        in_specs=[pl.BlockSpec((B, C, 512), lambda n: (0, n, Z_RQ // 512)),
                  pl.BlockSpec((B, C, 512), lambda n: (0, n, Z_RK // 512)),
                  pl.BlockSpec((B, C, 1024), lambda n: (0, n, Z_RV // 1024)),
                  pl.BlockSpec((B, C, 1024), lambda n: (0, n, Z_RG // 1024)),
                  pl.BlockSpec((C, LANES), lambda n: (n, 0)),
                  pl.BlockSpec((C, LANES), lambda n: (n, 0)),
                  pl.BlockSpec((RET_HEADS, LANES), lambda n: (0, 0))],
        out_specs=[pl.BlockSpec((B, C, 1024), lambda n: (0, n, 0)),
                   pl.BlockSpec((B, RET_HEADS, RET_DK, RET_DV), lambda n: (0, 0, 0, 0))],
        scratch_shapes=[pltpu.VMEM((B * RET_HEADS, RET_DK, RET_DV), F32)],
        compiler_params=_cp("arbitrary"),
        name="retention_prompt",
    )(z3, z3, z3, z3, cos_r, sin_r, lg)


def _ret_step_kernel(q_ref, k_ref, v_ref, rg_ref, s0_ref, cos_ref, sin_ref, lg_ref, o_ref, st_ref):
    cos, sin = cos_ref[...], sin_ref[...]
    eye = (lax.broadcasted_iota(I32, (RET_DK, RET_DK), 0) == lax.broadcasted_iota(I32, (RET_DK, RET_DK), 1))
    for h in range(RET_HEADS):
        qs = slice(h * RET_DK, (h + 1) * RET_DK)
        vs = slice(h * RET_DV, (h + 1) * RET_DV)
        q, k, v = q_ref[:, qs], k_ref[:, qs], v_ref[:, vs]
        qr = q * cos + pltpu.roll(q, RET_DK // 2, 1) * sin
        kr = (k * cos + pltpu.roll(k, RET_DK // 2, 1) * sin) * (RET_DK ** -0.5)
        lg = lg_ref[h:h + 1, :]
        gamma = jnp.exp(jnp.concatenate([lg, lg], axis=1))
        qcol = jnp.sum(jnp.where(eye, qr, 0.0), axis=1, keepdims=True)
        kcol = jnp.sum(jnp.where(eye, kr, 0.0), axis=1, keepdims=True)
        s0 = s0_ref[h]
        qk = jnp.sum(qr * kr, axis=1, keepdims=True)
        out = qk * v + jnp.sum(qcol * s0, axis=0, keepdims=True) * gamma
        st_ref[h] = gamma * s0 + kcol * v
        o_ref[:, vs] = (_silu(rg_ref[:, vs]) * _rms(out)).astype(BF16)


def _retention_sample(z3, state_ret, layer, cos_r, sin_r, lg):
    Bs = z3.shape[0]
    zc = lambda w, off: pl.BlockSpec((None, 1, w), lambda b, o=off // w: (b, 0, o))
    return pl.pallas_call(
        _ret_step_kernel,
        out_shape=[jax.ShapeDtypeStruct((Bs, 1, 1024), BF16),
                   jax.ShapeDtypeStruct((Bs, RET_HEADS, RET_DK, RET_DV), F32)],
        grid=(Bs,),
        in_specs=[zc(512, Z_RQ), zc(512, Z_RK), zc(1024, Z_RV), zc(1024, Z_RG),
                  pl.BlockSpec((None, None, RET_HEADS, RET_DK, RET_DV), lambda b: (layer, b, 0, 0, 0)),
                  pl.BlockSpec((1, LANES), lambda b: (0, 0)),
                  pl.BlockSpec((1, LANES), lambda b: (0, 0)),
                  pl.BlockSpec((RET_HEADS, LANES), lambda b: (0, 0))],
        out_specs=[pl.BlockSpec((None, 1, 1024), lambda b: (b, 0, 0)),
                   pl.BlockSpec((None, RET_HEADS, RET_DK, RET_DV), lambda b: (b, 0, 0, 0))],
        compiler_params=_cp("parallel"),
        name="retention_sample",
    )(z3, z3, z3, z3, state_ret, cos_r, sin_r, lg)


CONV_HALO = 32
CONV_RB = 32
CONV_CB = 512


def _conv_kernel(a_ref, g_ref, w_ref, b_ref, cg_ref, o_ref, cs_ref, u_ref, y_ref):
    t = pl.program_id(1)
    tm = a_ref.shape[0]

    @pl.when(t == 0)
    def _():
        u_ref[0:CONV_HALO, :] = jnp.zeros((CONV_HALO, u_ref.shape[1]), F32)

    u_ref[CONV_HALO:CONV_HALO + tm, :] = a_ref[...] * jax.nn.sigmoid(g_ref[...])
    first = CONV_HALO - (CONV_W - 1)
    for r in range(tm // CONV_RB):
        for c in range(u_ref.shape[1] // CONV_CB):
            cs = slice(c * CONV_CB, (c + 1) * CONV_CB)
            acc = jnp.zeros((CONV_RB, CONV_CB), F32)
            for j in range(CONV_W):
                acc = acc + w_ref[j:j + 1, cs] * u_ref[pl.ds(r * CONV_RB + first + j, CONV_RB), cs]
            y_ref[r * CONV_RB:(r + 1) * CONV_RB, cs] = acc
    cy = y_ref[...] + b_ref[...]
    o_ref[...] = _silu(_rms(cy) * cg_ref[...]).astype(BF16)

    @pl.when(t == pl.num_programs(1) - 1)
    def _():
        cs_ref[...] = u_ref[tm + first:tm + CONV_HALO, :]

    u_ref[0:CONV_HALO, :] = u_ref[tm:tm + CONV_HALO, :]


def _conv_prompt(z, B, T, conv_w, conv_b, conv_g, layer, tm):
    M, C = B * T, 1024
    nt = T // tm
    return pl.pallas_call(
        _conv_kernel,
        out_shape=[jax.ShapeDtypeStruct((M, C), BF16), jax.ShapeDtypeStruct((B, CONV_W - 1, C), F32)],
        grid=(B, nt),
        in_specs=[pl.BlockSpec((tm, C), lambda b, t: (b * nt + t, Z_CU // C)),
                  pl.BlockSpec((tm, C), lambda b, t: (b * nt + t, Z_CG // C)),
                  pl.BlockSpec((None, CONV_W, C), lambda b, t: (layer, 0, 0)),
                  pl.BlockSpec((None, 1, C), lambda b, t: (layer, 0, 0)),
                  pl.BlockSpec((None, 1, C), lambda b, t: (layer, 0, 0))],
        out_specs=[pl.BlockSpec((tm, C), lambda b, t: (b * nt + t, 0)),
                   pl.BlockSpec((None, CONV_W - 1, C), lambda b, t: (b, 0, 0))],
        scratch_shapes=[pltpu.VMEM((tm + CONV_HALO, C), F32), pltpu.VMEM((tm, C), F32)],
        compiler_params=_cp("parallel", "arbitrary"),
        name="conv_prompt",
    )(z, z, conv_w, conv_b, conv_g)


def _conv_step_kernel(a_ref, g_ref, buf_ref, w_ref, b_ref, cg_ref, o_ref, cs_ref):
    u = a_ref[...] * jax.nn.sigmoid(g_ref[...])
    buf = buf_ref[...]
    y = jnp.sum(w_ref[0:CONV_W - 1, :] * buf, axis=0, keepdims=True) + w_ref[CONV_W - 1:CONV_W, :] * u
    cy = y + b_ref[...]
    o_ref[...] = _silu(_rms(cy) * cg_ref[...]).astype(BF16)
    cs_ref[0:CONV_W - 2, :] = buf[1:CONV_W - 1, :]
    cs_ref[CONV_W - 2:CONV_W - 1, :] = u


def _conv_sample(z3, state_conv, conv_w, conv_b, conv_g, layer):
    Bs, C = z3.shape[0], 1024
    return pl.pallas_call(
        _conv_step_kernel,
        out_shape=[jax.ShapeDtypeStruct((Bs, 1, C), BF16), jax.ShapeDtypeStruct((Bs, CONV_W - 1, C), F32)],
        grid=(Bs,),
        in_specs=[pl.BlockSpec((None, 1, C), lambda b: (b, 0, Z_CU // C)),
                  pl.BlockSpec((None, 1, C), lambda b: (b, 0, Z_CG // C)),
                  pl.BlockSpec((None, None, CONV_W - 1, C), lambda b: (layer, b, 0, 0)),
                  pl.BlockSpec((None, CONV_W, C), lambda b: (layer, 0, 0)),
                  pl.BlockSpec((None, 1, C), lambda b: (layer, 0, 0)),
                  pl.BlockSpec((None, 1, C), lambda b: (layer, 0, 0))],
        out_specs=[pl.BlockSpec((None, 1, C), lambda b: (b, 0, 0)),
                   pl.BlockSpec((None, CONV_W - 1, C), lambda b: (b, 0, 0))],
        compiler_params=_cp("parallel"),
        name="conv_sample",
    )(z3, z3, state_conv, conv_w, conv_b, conv_g)


def _sel_kernel(ik3_ref, iqh_ref, iql_ref, iw_ref, m_ref, key_ref, w_ref, j_ref, *, T, K, ck):
    i = pl.program_id(1)
    nq = LANES
    for h in range(IDX_HEADS):
        rs = slice(h * IDX_DIM, (h + 1) * IDX_DIM)
        cs = slice(h * nq, (h + 1) * nq)
        hi = iqh_ref[rs, :]
        w_ref[0:IDX_DIM, cs] = hi
        w_ref[IDX_DIM:2 * IDX_DIM, cs] = hi
        w_ref[2 * IDX_DIM:3 * IDX_DIM, cs] = iql_ref[rs, :]
        w_ref[3 * IDX_DIM:4 * IDX_DIM, cs] = jnp.zeros((IDX_DIM, nq), BF16)
    nch = ((i + 1) * nq + ck - 1) // ck
    tq = i * nq + lax.broadcasted_iota(I32, (ck, nq), 1)
    row = lax.broadcasted_iota(I32, (ck, nq), 0)
    iw = iw_ref[...]

    def score_body(c, carry):
        off = pl.multiple_of(c * ck, ck)
        d = _dot(ik3_ref[pl.ds(off, ck), :], w_ref[...])
        s = jnp.zeros((ck, nq), F32)
        for h in range(IDX_HEADS):
            s = s + jnp.maximum(d[:, h * nq:(h + 1) * nq], 0.0) * iw[h:h + 1, :]
        key_ref[pl.ds(off, ck), :] = jnp.where(off + row <= tq, _sortable(s), INT_MIN)
        return carry

    lax.fori_loop(0, nch, score_body, 0)

    def count(pred):
        def body(c, acc):
            off = pl.multiple_of(c * ck, ck)
            hit = jnp.where(pred(key_ref[pl.ds(off, ck), :], off), 1.0, 0.0)
            return acc + jnp.sum(hit.reshape(ck // 8, 8, nq), axis=0)
        acc = lax.fori_loop(0, nch, body, jnp.zeros((8, nq), F32))
        return jnp.sum(acc, axis=0, keepdims=True)

    tau = jnp.full((1, nq), INT_MIN, I32)
    for bit in range(31, -1, -1):
        cand = jnp.zeros((1, nq), I32) if bit == 31 else tau | (1 << bit)
        cnt = count(lambda kk, off, cand=cand: kk >= cand)
        tau = jnp.where(cnt >= K, cand, tau)
    tau = jnp.maximum(tau, INT_MIN + 1)
    n_gt = count(lambda kk, off: kk > tau)
    n_ge = count(lambda kk, off: kk >= tau)
    need = K - n_gt
    j_ref[...] = jnp.full((1, nq), 2 ** 30, I32)

    @pl.when(jnp.max(n_ge) > K)
    def _():
        jj = jnp.zeros((1, nq), I32)
        for bit in range(T.bit_length() - 1, -1, -1):
            cand = jj | (1 << bit)
            f = count(lambda kk, off, cand=cand: jnp.where(kk == tau, off + row, 2 ** 30) < cand)
            jj = jnp.where(f <= need, cand, jj)
        j_ref[...] = jj

    jlim = j_ref[...]

    def write_body(c, carry):
        off = pl.multiple_of(c * ck, ck)
        kk = key_ref[pl.ds(off, ck), :]
        tie_pos = jnp.where(kk == tau, off + row, 2 ** 30)
        sel = jnp.where(kk > tau, 1.0, jnp.where(tie_pos < jlim, 1.0, 0.0))
        m_ref[pl.ds(off, ck), :] = sel.astype(BF16)
        return carry

    lax.fori_loop(0, nch, write_body, 0)

    def zero_body(c, carry):
        off = pl.multiple_of(c * ck, ck)
        m_ref[pl.ds(off, ck), :] = jnp.zeros((ck, nq), BF16)
        return carry

    lax.fori_loop(nch, T // ck, zero_body, 0)


def _select_prompt(ik3, iqth, iqtl, iwt, B, T, ck):
    nq = T // LANES
    K = min(TOPK_MAX, T // 4)
    return pl.pallas_call(
        functools.partial(_sel_kernel, T=T, K=K, ck=ck),
        out_shape=jax.ShapeDtypeStruct((B, T, T), BF16),
        grid=(B, nq),
        in_specs=[pl.BlockSpec((T, 256), lambda b, i: (b, 0)),
                  pl.BlockSpec((512, LANES), lambda b, i: (0, b * nq + i)),
                  pl.BlockSpec((512, LANES), lambda b, i: (0, b * nq + i)),
                  pl.BlockSpec((IDX_HEADS, LANES), lambda b, i: (0, b * nq + i))],
        out_specs=pl.BlockSpec((None, T, LANES), lambda b, i: (b, 0, i)),
        scratch_shapes=[pltpu.VMEM((T, LANES), I32), pltpu.VMEM((256, IDX_HEADS * LANES), BF16),
                        pltpu.VMEM((1, LANES), I32)],
        compiler_params=_cp("parallel", "arbitrary"),
        name="dsa_select_prompt",
    )(ik3, iqth, iqtl, iwt)


def _att_kernel(q_ref, k_ref, vt_ref, m_ref, o_ref, *, ck):
    i = pl.program_id(2)
    nq = LANES
    nch = ((i + 1) * nq + ck - 1) // ck
    q2 = q_ref[...]
    qs = jnp.concatenate([q2[:, :HEAD_DIM], q2[:, HEAD_DIM:]], axis=0)

    def body(c, carry):
        m, l, acc = carry
        off = pl.multiple_of(c * ck, ck)
        st = _dot_nt(k_ref[pl.ds(off, ck), :], qs)
        mf = m_ref[pl.ds(off, ck), :].astype(F32)
        sel = jnp.concatenate([mf, mf], axis=1) > 0.0
        sm = jnp.where(sel, st, NEG)
        m_new = jnp.maximum(m, jnp.max(sm, axis=0, keepdims=True))
        alpha = jnp.exp(m - m_new)
        p = jnp.where(sel, jnp.exp(sm - m_new), 0.0)
        l = l * alpha + jnp.sum(p, axis=0, keepdims=True)
        acc = acc * alpha + _dot(vt_ref[c], p.astype(BF16))
        return m_new, l, acc

    init = (jnp.full((1, 2 * nq), NEG, F32), jnp.zeros((1, 2 * nq), F32), jnp.zeros((HEAD_DIM, 2 * nq), F32))
    _, l, acc = lax.fori_loop(0, nch, body, init)
    ot = acc / l
    o_ref[:, 0:HEAD_DIM] = ot[:, 0:nq].T.astype(BF16)
    o_ref[:, HEAD_DIM:2 * HEAD_DIM] = ot[:, nq:2 * nq].T.astype(BF16)


def _attend_prompt(qa, kb, vt3, mask, B, T, ck):
    nq = T // LANES
    M = B * T
    return pl.pallas_call(
        functools.partial(_att_kernel, ck=ck),
        out_shape=jax.ShapeDtypeStruct((M, ATT_HEADS * HEAD_DIM), BF16),
        grid=(B, KV_HEADS, nq),
        in_specs=[pl.BlockSpec((LANES, 2 * HEAD_DIM), lambda b, g, i: (b * nq + i, g)),
                  pl.BlockSpec((T, HEAD_DIM), lambda b, g, i: (b, g)),
                  pl.BlockSpec((T // ck, HEAD_DIM, ck), lambda b, g, i: (b, g, 0)),
                  pl.BlockSpec((None, T, LANES), lambda b, g, i: (b, 0, i))],
        out_specs=pl.BlockSpec((LANES, 2 * HEAD_DIM), lambda b, g, i: (b * nq + i, g)),
        compiler_params=_cp("parallel", "parallel", "arbitrary"),
        name="dsa_attend_prompt",
    )(qa, kb, vt3, mask)


def _sel_step_kernel(pt_ref, iq_ref, iw_ref, iko_ref, *refs, pps, K, n_pages):
    ik_refs = refs[:pps]
    msk_ref, own_ref, sc_ref = refs[pps:]
    p = pl.program_id(1)
    iq = iq_ref[...]
    iw = iw_ref[...]
    for j in range(pps):
        d = _dot3(iq, ik_refs[j][...], nt=True)
        sc_ref[pl.ds(p * pps + j, 1), :] = jnp.sum(jnp.maximum(d, 0.0) * iw, axis=0, keepdims=True)

    @pl.when(p == pl.num_programs(1) - 1)
    def _():
        key = _sortable(sc_ref[...])
        d_own = jnp.sum(iq * iko_ref[...], axis=1, keepdims=True)
        s_own = jnp.sum(jnp.maximum(d_own, 0.0) * iw, axis=0, keepdims=True)
        key_own = _sortable(s_own)
        pos = (lax.broadcasted_iota(I32, key.shape, 0) * PAGE + lax.broadcasted_iota(I32, key.shape, 1))
        pos_own = n_pages * PAGE

        def count(pred):
            c = jnp.sum(jnp.where(pred(key, pos), 1.0, 0.0), axis=1, keepdims=True)
            c = jnp.sum(c, axis=0, keepdims=True)
            return c + jnp.where(pred(key_own, pos_own), 1.0, 0.0)

        tau = jnp.full((1, 1), INT_MIN, I32)
        for bit in range(31, -1, -1):
            cand = jnp.zeros((1, 1), I32) if bit == 31 else tau | (1 << bit)
            tau = jnp.where(count(lambda kk, ps, cand=cand: kk >= cand) >= K, cand, tau)
        tau = jnp.maximum(tau, INT_MIN + 1)
        need = K - count(lambda kk, ps: kk > tau)
        jj = jnp.zeros((1, 1), I32)
        for bit in range((pos_own + 1).bit_length() - 1, -1, -1):
            cand = jj | (1 << bit)
            f = count(lambda kk, ps, cand=cand: jnp.where(kk == tau, ps, 2 ** 30) < cand)
            jj = jnp.where(f <= need, cand, jj)

        def selected(kk, ps):
            return jnp.where(kk > tau, 1.0, jnp.where(jnp.where(kk == tau, ps, 2 ** 30) < jj, 1.0, 0.0))

        msk_ref[...] = selected(key, pos)
        own_ref[...] = jnp.broadcast_to(selected(key_own, pos_own), own_ref.shape)


def _select_sample(page_table, iq8, iw8, ik_own, cache_idx_k, layer, pps):
    Bs, n_pages = page_table.shape
    K = min(TOPK_MAX, (n_pages * PAGE + 1) // 4)
    page_spec = lambda j: pl.BlockSpec((None, None, PAGE, IDX_DIM),
                                       lambda b, p, pt, j=j: (layer, pt[b, p * pps + j], 0, 0))
    grid_spec = pltpu.PrefetchScalarGridSpec(
        num_scalar_prefetch=1,
        grid=(Bs, n_pages // pps),
        in_specs=[pl.BlockSpec((None, IDX_HEADS, IDX_DIM), lambda b, p, pt: (b, 0, 0)),
                  pl.BlockSpec((None, IDX_HEADS, 1), lambda b, p, pt: (b, 0, 0)),
                  pl.BlockSpec((None, 1, IDX_DIM), lambda b, p, pt: (b, 0, 0))]
                 + [page_spec(j) for j in range(pps)],
        out_specs=[pl.BlockSpec((None, n_pages, PAGE), lambda b, p, pt: (b, 0, 0)),
                   pl.BlockSpec((None, 1, LANES), lambda b, p, pt: (b, 0, 0))],
        scratch_shapes=[pltpu.VMEM((n_pages, PAGE), F32)],
    )
    return pl.pallas_call(
        functools.partial(_sel_step_kernel, pps=pps, K=K, n_pages=n_pages),
        out_shape=[jax.ShapeDtypeStruct((Bs, n_pages, PAGE), F32), jax.ShapeDtypeStruct((Bs, 1, LANES), F32)],
        grid_spec=grid_spec,
        compiler_params=_cp("parallel", "arbitrary"),
        name="dsa_select_sample",
    )(page_table, iq8, iw8, ik_own, *([cache_idx_k] * pps))


def _att_step_kernel(pt_ref, q_ref, msk_ref, own_ref, ko_ref, vo_ref, *refs, pp):
    k_refs, v_refs = refs[:pp], refs[pp:2 * pp]
    o_ref, m_sc, l_sc, acc_sc = refs[2 * pp:]
    p = pl.program_id(1)
    kvw = KV_HEADS * HEAD_DIM

    @pl.when(p == 0)
    def _():
        m_sc[...] = jnp.full(m_sc.shape, NEG, F32)
        l_sc[...] = jnp.zeros(l_sc.shape, F32)
        acc_sc[...] = jnp.zeros(acc_sc.shape, F32)

    q8 = q_ref[...].astype(F32)
    grp = ATT_HEADS // KV_HEADS
    blk = (lax.broadcasted_iota(I32, (ATT_HEADS, kvw), 1) // HEAD_DIM
           == lax.broadcasted_iota(I32, (ATT_HEADS, kvw), 0) // grp)
    qb = jnp.where(blk, jnp.concatenate([q8] * KV_HEADS, axis=1), 0.0)
    qbb = qb.astype(BF16)
    m, l, acc = m_sc[...], l_sc[...], acc_sc[...]
    for j in range(pp):
        s = _dot_nt(qbb, k_refs[j][...].astype(BF16))
        sel = jnp.broadcast_to(msk_ref[pl.ds(p * pp + j, 1), :], s.shape) > 0.0
        sm = jnp.where(sel, s, NEG)
        m_new = jnp.maximum(m, jnp.max(sm, axis=1, keepdims=True))
        alpha = jnp.exp(m - m_new)
        pr = jnp.where(sel, jnp.exp(sm - m_new), 0.0)
        l = l * alpha + jnp.sum(pr, axis=1, keepdims=True)
        acc = acc * alpha + _dot(pr.astype(BF16), v_refs[j][...].astype(BF16))
        m = m_new
    m_sc[...], l_sc[...], acc_sc[...] = m, l, acc

    @pl.when(p == pl.num_programs(1) - 1)
    def _():
        ko = ko_ref[...].astype(BF16).astype(F32)
        s_own = jnp.sum(qb * ko, axis=1, keepdims=True)
        sel_own = own_ref[:, 0:1] > 0.0
        sm = jnp.where(sel_own, s_own, NEG)
        m_new = jnp.maximum(m, sm)
        alpha = jnp.exp(m - m_new)
        pr = jnp.where(sel_own, jnp.exp(sm - m_new), 0.0)
        lf = l * alpha + pr
        o = (acc * alpha + pr * vo_ref[...]) / lf
        o = jnp.where(blk, o, 0.0)
        o8 = o[:, 0:HEAD_DIM]
        for g in range(1, KV_HEADS):
            o8 = o8 + o[:, g * HEAD_DIM:(g + 1) * HEAD_DIM]
        o_ref[...] = o8.astype(BF16)


def _attend_sample(page_table, q8, msk, own, k_own, v_own, cache_k, cache_v, layer, pp):
    Bs, n_pages = page_table.shape
    kvw = KV_HEADS * HEAD_DIM
    page_spec = lambda j: pl.BlockSpec((None, None, PAGE, kvw),
                                       lambda b, p, pt, j=j: (layer, pt[b, p * pp + j], 0, 0))
    per_b = lambda r, w: pl.BlockSpec((None, r, w), lambda b, p, pt: (b, 0, 0))
    grid_spec = pltpu.PrefetchScalarGridSpec(
        num_scalar_prefetch=1,
        grid=(Bs, n_pages // pp),
        in_specs=[per_b(ATT_HEADS, HEAD_DIM), per_b(n_pages, PAGE), per_b(1, LANES), per_b(1, kvw), per_b(1, kvw)]
                 + [page_spec(j) for j in range(pp)] + [page_spec(j) for j in range(pp)],
        out_specs=per_b(ATT_HEADS, HEAD_DIM),
        scratch_shapes=[pltpu.VMEM((ATT_HEADS, 1), F32), pltpu.VMEM((ATT_HEADS, 1), F32),
                        pltpu.VMEM((ATT_HEADS, kvw), F32)],
    )
    return pl.pallas_call(
        functools.partial(_att_step_kernel, pp=pp),
        out_shape=jax.ShapeDtypeStruct((Bs, ATT_HEADS, HEAD_DIM), BF16),
        grid_spec=grid_spec,
        compiler_params=_cp("parallel", "arbitrary"),
        name="dsa_attend_sample",
    )(page_table, q8, msk, own, k_own, v_own, *([cache_k] * pp), *([cache_v] * pp))


def _mixout_kernel(x_ref, ro_ref, ca_ref, ao_ref, g0_ref, g1_ref, g2_ref, gm_ref,
                   wr_ref, wc_ref, wa_ref, wo_ref, o_ref):
    merged = (jax.nn.sigmoid(g0_ref[...]) * _dot(ro_ref[...], wr_ref[...])
              + jax.nn.sigmoid(g1_ref[...]) * _dot(ca_ref[...], wc_ref[...])
              + jax.nn.sigmoid(g2_ref[...]) * _dot(ao_ref[...], wa_ref[...]))
    o_ref[...] = x_ref[...] + gm_ref[0] * _dot(merged.astype(BF16), wo_ref[...])


def _mixout(x, ro, ca, ao, z, gate, w_r, w_c, w_a, w_o, tm, rpg):
    M, D = x.shape
    row = pl.BlockSpec((tm, D), lambda i: (i, 0))
    glc = lambda k: pl.BlockSpec((tm, D), lambda i, o=Z_GL // D + k: (i, o))
    wsp = pl.BlockSpec((D, D), lambda i: (0, 0))
    return pl.pallas_call(
        _mixout_kernel,
        out_shape=jax.ShapeDtypeStruct((M, D), F32),
        grid=(M // tm,),
        in_specs=[row, row, row, row, glc(0), glc(1), glc(2), _mod_spec(gate, tm, rpg), wsp, wsp, wsp, wsp],
        out_specs=row,
        compiler_params=_cp("parallel"),
        name="mixer_out",
    )(x, ro, ca, ao, z, z, z, gate, w_r, w_c, w_a, w_o)


def _rope_tables(pos, rot_dim, theta, period):
    half = rot_dim // 2
    inv = 1.0 / (theta ** (jnp.arange(half, dtype=F32) / half))
    ang = pos.astype(F32)[:, None] * inv[None, :]
    lane = np.arange(LANES) % period
    idx = lane % half
    cos = jnp.where(lane < rot_dim, jnp.cos(ang)[:, idx], 1.0)
    sin = jnp.sin(ang)[:, idx]
    s_lo = jnp.where(lane < half, -sin, 0.0)
    s_hi = jnp.where((lane >= half) & (lane < rot_dim), sin, 0.0)
    return cos, s_lo, s_hi


def _tables(pos):
    ca = _rope_tables(pos, ROT_DIM, ROPE_THETA, HEAD_DIM)
    ci = _rope_tables(pos, IDX_ROT, ROPE_THETA, IDX_DIM)
    cr, r_lo, r_hi = _rope_tables(pos, RET_DK, RET_THETA, RET_DK)
    return ca, ci, (cr, r_lo + r_hi)


def _mods(mod, rows):
    parts = jnp.split(mod, 9, axis=-1)
    if rows == 1:
        return [p[:, None, :] for p in parts]
    return [p[None] for p in parts]


def _cat_w_in(w):
    pad = Z_END - Z_IX - (IDX_DIM + IDX_HEADS)
    return jnp.concatenate([w[:, :6144], w[:, 7752:10824], w[:, 6144:7680], w[:, 7680:7752],
                            jnp.zeros((w.shape[0], pad), w.dtype)], axis=1).astype(BF16)


def kernel(x_prompt, x_sample, cache_k, cache_v, cache_idx_k, state_ret, state_conv, page_table, c_prompt, c_sample, w_ada, b_ada, norm_g, w_ff1_up, w_ff1_down, w_ff2_up, w_ff2_down, w_in, conv_w, conv_b, conv_g, q_norm_g, k_norm_g, idx_k_norm_g, w_ret_o, w_conv_o, w_att_o, w_o):
    B, T, D = x_prompt.shape
    Bs, Ts, _ = x_sample.shape
    assert Ts == 1 and T % 512 == 0 and D == 1024
    depth = w_ada.shape[0]
    n_pages = page_table.shape[1]
    past = n_pages * PAGE
    tm_p = 512
    ck = 256
    pps = 8 if n_pages % 8 == 0 else 4
    pp = 4

    tabs_p = _tables(jnp.arange(T, dtype=jnp.int32))
    tabs_s = _tables(jnp.full((Bs,), past, dtype=jnp.int32))
    lg = jnp.broadcast_to(jnp.log(1.0 - 2.0 ** (-5.0 - jnp.arange(RET_HEADS, dtype=F32)))[:, None],
                          (RET_HEADS, LANES))
    c_all = jnp.concatenate([c_prompt, c_sample], axis=0)
    kvw = KV_HEADS * HEAD_DIM
    ck4 = cache_k.reshape(cache_k.shape[0], cache_k.shape[1], PAGE, kvw)
    cv4 = cache_v.reshape(cache_v.shape[0], cache_v.shape[1], PAGE, kvw)
    conv_b3 = conv_b.reshape(depth, 1, D)
    conv_g3 = conv_g.reshape(depth, 1, D)

    xp = x_prompt.reshape(B * T, D)
    xs = x_sample.reshape(Bs, D)
    outs_p, outs_s = [], []
    for l in range(depth):
        mod = _ada_mod(c_all, w_ada, b_ada, l)
        mp = _mods(mod[:B], 1)
        msm = _mods(mod[B:], Bs)
        wu1, wd1 = w_ff1_up[l].astype(BF16), w_ff1_down[l].astype(BF16)
        wu2, wd2 = w_ff2_up[l].astype(BF16), w_ff2_down[l].astype(BF16)
        w_cat = _cat_w_in(w_in[l])
        w_r, w_c = w_ret_o[l].astype(BF16), w_conv_o[l].astype(BF16)
        w_a, w_oo = w_att_o[l].astype(BF16), w_o[l].astype(BF16)
        ng = norm_g[l]
        qg, kg = q_norm_g[l][None, :], k_norm_g[l][None, :]
        ikg = jnp.concatenate([idx_k_norm_g[l], jnp.zeros((LANES - IDX_DIM,), F32)])[None, :]

        xp = _ffn(xp, mp[0], mp[1], mp[2], ng[0:1], wu1, wd1, tm_p, T)
        z = _inproj(xp, mp[3], mp[4], ng[1:2], w_cat, min(1024, T), T)
        (qa, k_p, kb, v_p, _iqr, _ixo, idxk_p, vt3, iqth, iqtl, ik3, iwt) = _prep(
            z, tabs_p[0], tabs_p[1], qg, kg, ikg, tm_p, True, ck)
        ro, ret_p = _retention_prompt(z.reshape(B, T, Z_END), tabs_p[2][0], tabs_p[2][1], lg)
        ca, conv_p = _conv_prompt(z, B, T, conv_w, conv_b3, conv_g3, l, tm_p)
        mask = _select_prompt(ik3, iqth, iqtl, iwt, B, T, ck)
        ao = _attend_prompt(qa, kb, vt3, mask, B, T, ck)
        xp = _mixout(xp, ro.reshape(B * T, D), ca, ao, z, mp[5], w_r, w_c, w_a, w_oo, tm_p, T)
        xp = _ffn(xp, mp[6], mp[7], mp[8], ng[2:3], wu2, wd2, tm_p, T)
        outs_p.append((k_p.reshape(B, T, KV_HEADS, HEAD_DIM), v_p.reshape(B, T, KV_HEADS, HEAD_DIM),
                       idxk_p.reshape(B, T, IDX_DIM), ret_p, conv_p))

        xs = _ffn(xs, msm[0], msm[1], msm[2], ng[0:1], wu1, wd1, Bs, Bs)
        zs = _inproj(xs, msm[3], msm[4], ng[1:2], w_cat, Bs, Bs)
        qa_s, k_s, _kb, v_s, iqr_s, ixo_s, idxk_s = _prep(
            zs, tabs_s[0], tabs_s[1], qg, kg, ikg, Bs, False, ck)
        zs3 = zs.reshape(Bs, 1, Z_END)
        ro_s, ret_s = _retention_sample(zs3, state_ret, l, tabs_s[2][0][0:1], tabs_s[2][1][0:1], lg)
        ca_s, conv_s = _conv_sample(zs3, state_conv, conv_w, conv_b3, conv_g3, l)
        iq8 = iqr_s.reshape(Bs, IDX_HEADS, IDX_DIM)
        iw8 = ixo_s[:, IDX_DIM:IDX_DIM + IDX_HEADS].reshape(Bs, IDX_HEADS, 1)
        msk, own = _select_sample(page_table, iq8, iw8, idxk_s.reshape(Bs, 1, IDX_DIM), cache_idx_k, l, pps)
        ao_s = _attend_sample(page_table, qa_s.reshape(Bs, ATT_HEADS, HEAD_DIM), msk, own,
                              k_s.reshape(Bs, 1, kvw), v_s.reshape(Bs, 1, kvw), ck4, cv4, l, pp)
        xs = _mixout(xs, ro_s.reshape(Bs, D), ca_s.reshape(Bs, D), ao_s.reshape(Bs, D), zs, msm[5],
                     w_r, w_c, w_a, w_oo, Bs, Bs)
        xs = _ffn(xs, msm[6], msm[7], msm[8], ng[2:3], wu2, wd2, Bs, Bs)
        outs_s.append((k_s.reshape(Bs, 1, KV_HEADS, HEAD_DIM), v_s.reshape(Bs, 1, KV_HEADS, HEAD_DIM),
                       idxk_s.reshape(Bs, 1, IDX_DIM), ret_s, conv_s))

    stack = lambda outs, j: jnp.stack([o[j] for o in outs])
    return (xp.reshape(B, T, D), xs.reshape(Bs, 1, D),
            stack(outs_p, 0), stack(outs_p, 1), stack(outs_p, 2), stack(outs_p, 3), stack(outs_p, 4),
            stack(outs_s, 0), stack(outs_s, 1), stack(outs_s, 2), stack(outs_s, 3), stack(outs_s, 4))
```

```python
import functools

import numpy as np
import jax
import jax.numpy as jnp
from jax import lax
from jax.experimental import pallas as pl
from jax.experimental.pallas import tpu as pltpu

F32 = jnp.float32
BF16 = jnp.bfloat16
I32 = jnp.int32

EPS = 1e-6
LANES = 128
NEG = -1e30
MASK_BIAS = -1e30
M_INIT = -1e20
INT_MIN = -2 ** 31
COUNT_ROWS = 64

RET_HEADS = 4
RET_DK = 128
RET_DV = 256
RET_CHUNK = 128
RET_THETA = 10000.0
CONV_W = 31
ATT_HEADS = 8
KV_HEADS = 4
HEAD_DIM = 128
IDX_HEADS = 8
IDX_DIM = 64
TOPK_MAX = 256
ROPE_THETA = 500000.0
ROT_DIM = HEAD_DIM // 4
IDX_ROT = IDX_DIM // 4
PAGE = 128

Z_RQ, Z_RK, Z_RV, Z_RG = 0, 512, 1024, 2048
Z_CU, Z_CG, Z_AQ, Z_GL = 3072, 4096, 5120, 6144
Z_AK, Z_AV, Z_IQ, Z_IX, Z_END = 9216, 9728, 10240, 10752, 10880

VMEM_LIMIT = 56 * 1024 * 1024


def _cp(*sem):
    return pltpu.CompilerParams(dimension_semantics=tuple(sem), vmem_limit_bytes=VMEM_LIMIT)


def _dot(a, b):
    return jnp.dot(a, b, preferred_element_type=F32)


def _dot_nt(a, b):
    return lax.dot_general(a, b, (((1,), (1,)), ((), ())), preferred_element_type=F32)


def _split(x):
    hi = x.astype(BF16)
    lo = (x - hi.astype(F32)).astype(BF16)
    return hi, lo


def _dot3(a, b, nt=False):
    d = _dot_nt if nt else _dot
    ah, al = _split(a)
    bh, bl = _split(b)
    return d(ah, bh) + d(ah, bl) + d(al, bh)


def _silu(x):
    return x * jax.nn.sigmoid(x)


def _rms(x):
    return x * lax.rsqrt(jnp.mean(x * x, axis=-1, keepdims=True) + EPS)


def _norm_mod(x, g, shift, scale):
    return _rms(x) * g * (1.0 + scale) + shift


def _rot(x, cos, s_lo, s_hi, half):
    return x * cos + pltpu.roll(x, LANES - half, 1) * s_lo + pltpu.roll(x, half, 1) * s_hi


def _sortable(s):
    s = jnp.where(s == 0.0, 0.0, s)
    bits = pltpu.bitcast(s, I32)
    return bits ^ ((bits >> 31) & 0x7FFFFFFF)


def _ada_kernel(c_ref, w_ref, b_ref, o_ref):
    o_ref[...] = _dot3(_silu(c_ref[...]), w_ref[...]) + b_ref[...]


def _ada_mod(c, w_ada, b_ada, layer):
    R, D = c.shape
    N = w_ada.shape[2]
    tn = 1024
    return pl.pallas_call(
        _ada_kernel,
        out_shape=jax.ShapeDtypeStruct((R, N), F32),
        grid=(N // tn,),
        in_specs=[pl.BlockSpec((R, D), lambda j: (0, 0)),
                  pl.BlockSpec((None, D, tn), lambda j: (layer, 0, j)),
                  pl.BlockSpec((None, 1, tn), lambda j: (layer, 0, j))],
        out_specs=pl.BlockSpec((R, tn), lambda j: (0, j)),
        compiler_params=_cp("parallel"),
        name="ada_mod",
    )(c, w_ada, b_ada.reshape(b_ada.shape[0], 1, N))


def _ffn_kernel(x_ref, sh_ref, sc_ref, g_ref, ng_ref, wu_ref, wd_ref, o_ref, acc_ref, *, ff, tf):
    x = x_ref[...]
    h = _norm_mod(x, ng_ref[...], sh_ref[0], sc_ref[0]).astype(BF16)
    for j in range(ff // tf):
        a = _dot(h, wu_ref[:, j * tf:(j + 1) * tf])
        b = _dot(h, wu_ref[:, ff + j * tf:ff + (j + 1) * tf])
        act = (_silu(a) * b).astype(BF16)
        upd = _dot(act, wd_ref[j * tf:(j + 1) * tf, :])
        if j == 0:
            acc_ref[...] = upd
        else:
            acc_ref[...] += upd
    o_ref[...] = x + 0.5 * g_ref[0] * acc_ref[...]


def _mod_spec(mod, tm, rpg):
    rm = mod.shape[1]
    return pl.BlockSpec((1, rm, mod.shape[2]), lambda i: ((i * tm) // rpg, 0, 0))


def _ffn(x, shift, scale, gate, ng, w_up, w_down, tm, rpg):
    M, D = x.shape
    ff = w_down.shape[0]
    const = lambda i: (0, 0)
    return pl.pallas_call(
        functools.partial(_ffn_kernel, ff=ff, tf=256),
        out_shape=jax.ShapeDtypeStruct((M, D), F32),
        grid=(M // tm,),
        in_specs=[pl.BlockSpec((tm, D), lambda i: (i, 0)),
                  _mod_spec(shift, tm, rpg), _mod_spec(scale, tm, rpg), _mod_spec(gate, tm, rpg),
                  pl.BlockSpec((1, D), const),
                  pl.BlockSpec((D, 2 * ff), const),
                  pl.BlockSpec((ff, D), const)],
        out_specs=pl.BlockSpec((tm, D), lambda i: (i, 0)),
        scratch_shapes=[pltpu.VMEM((tm, D), F32)],
        compiler_params=_cp("parallel"),
        name="ffn",
    )(x, shift, scale, gate, ng, w_up, w_down)


def _inproj_kernel(x_ref, sh_ref, sc_ref, ng_ref, w_ref, o_ref, h_ref):
    @pl.when(pl.program_id(1) == 0)
    def _():
        h_ref[...] = _norm_mod(x_ref[...], ng_ref[...], sh_ref[0], sc_ref[0]).astype(BF16)

    o_ref[...] = _dot(h_ref[...], w_ref[...])


def _inproj(x, shift, scale, ng, w_cat, tm, rpg):
    M, D = x.shape
    N = w_cat.shape[1]
    tn = N // 5
    ms = lambda mod: pl.BlockSpec((1, mod.shape[1], D), lambda i, j: ((i * tm) // rpg, 0, 0))
    return pl.pallas_call(
        _inproj_kernel,
        out_shape=jax.ShapeDtypeStruct((M, N), F32),
        grid=(M // tm, N // tn),
        in_specs=[pl.BlockSpec((tm, D), lambda i, j: (i, 0)),
                  ms(shift), ms(scale),
                  pl.BlockSpec((1, D), lambda i, j: (0, 0)),
                  pl.BlockSpec((D, tn), lambda i, j: (0, j))],
        out_specs=pl.BlockSpec((tm, tn), lambda i, j: (i, j)),
        scratch_shapes=[pltpu.VMEM((tm, D), BF16)],
        compiler_params=_cp("parallel", "arbitrary"),
        name="inproj",
    )(x, shift, scale, ng, w_cat)


def _prep_kernel(aq_ref, ak_ref, av_ref, iq_ref, ix_ref,
                 ca_ref, sa1_ref, sa2_ref, ci_ref, si1_ref, si2_ref, qg_ref, kg_ref, ikg_ref,
                 qa_ref, k_ref, kb_ref, v_ref, iqr_ref, ixo_ref, idxk_ref, *t_refs, tck):
    ca, sa1, sa2 = ca_ref[...], sa1_ref[...], sa2_ref[...]
    ci, si1, si2 = ci_ref[...], si1_ref[...], si2_ref[...]
    qg, kg = qg_ref[...], kg_ref[...]
    q_scale = HEAD_DIM ** -0.5
    for h in range(ATT_HEADS):
        sl = slice(h * HEAD_DIM, (h + 1) * HEAD_DIM)
        y = _rot(_rms(aq_ref[:, sl]) * qg, ca, sa1, sa2, ROT_DIM // 2)
        qa_ref[:, sl] = (y * q_scale).astype(BF16)
    for h in range(KV_HEADS):
        sl = slice(h * HEAD_DIM, (h + 1) * HEAD_DIM)
        y = _rot(_rms(ak_ref[:, sl]) * kg, ca, sa1, sa2, ROT_DIM // 2)
        k_ref[:, sl] = y
        kb_ref[:, sl] = y.astype(BF16)
    av = av_ref[...]
    v_ref[...] = av
    for j in range(IDX_HEADS * IDX_DIM // LANES):
        sl = slice(j * LANES, (j + 1) * LANES)
        iqr_ref[:, sl] = _rot(iq_ref[:, sl], ci, si1, si2, IDX_ROT // 2)
    blk = ix_ref[...]
    lane = lax.broadcasted_iota(I32, blk.shape, 1)
    ikm = jnp.where(lane < IDX_DIM, blk, 0.0)
    ms = jnp.sum(ikm * ikm, axis=-1, keepdims=True) * (1.0 / IDX_DIM)
    ikr = _rot(ikm * lax.rsqrt(ms + EPS) * ikg_ref[...], ci, si1, si2, IDX_ROT // 2)
    iw_scale = IDX_HEADS ** -0.5 * IDX_DIM ** -0.5
    iws = jnp.where((lane >= IDX_DIM) & (lane < IDX_DIM + IDX_HEADS), blk * iw_scale, 0.0)
    ixo = ikr + iws
    ixo_ref[...] = ixo
    idxk_ref[...] = ikr[:, :IDX_DIM]
    if t_refs:
        vt_ref, iqth_ref, iqtl_ref, ik3_ref, iwt_ref = t_refs
        tm = av.shape[0]
        avt = av.T
        for c in range(tm // tck):
            vt_ref[c] = avt[:, c * tck:(c + 1) * tck].astype(BF16)
        iqt = iqr_ref[...].T
        hi = iqt.astype(BF16)
        iqth_ref[...] = hi
        iqtl_ref[...] = (iqt - hi.astype(F32)).astype(BF16)
        khi = ikr.astype(BF16).astype(F32)
        klo = ikr - khi
        ik3_ref[...] = jnp.concatenate([khi + pltpu.roll(klo, IDX_DIM, 1), khi], axis=1).astype(BF16)
        iwt_ref[...] = ixo.T[IDX_DIM:IDX_DIM + IDX_HEADS, :]


def _prep(z, tabs_a, tabs_i, qg, kg, ikg, tm, transposed, tck):
    M = z.shape[0]
    nt = M // tm
    tpb = tabs_a[0].shape[0] // tm
    tab = pl.BlockSpec((tm, LANES), lambda i: (i % tpb, 0))
    col = lambda w, off: pl.BlockSpec((tm, w), lambda i, o=off // w: (i, o))
    row = lambda w: pl.BlockSpec((tm, w), lambda i: (i, 0))
    vec = pl.BlockSpec((1, LANES), lambda i: (0, 0))
    in_specs = [col(1024, Z_AQ), col(512, Z_AK), col(512, Z_AV), col(512, Z_IQ), col(128, Z_IX)]
    in_specs += [tab] * 6 + [vec] * 3
    out_shape = [jax.ShapeDtypeStruct((M, 1024), BF16), jax.ShapeDtypeStruct((M, 512), F32),
                 jax.ShapeDtypeStruct((M, 512), BF16), jax.ShapeDtypeStruct((M, 512), F32),
                 jax.ShapeDtypeStruct((M, 512), F32), jax.ShapeDtypeStruct((M, LANES), F32),
                 jax.ShapeDtypeStruct((M, IDX_DIM), F32)]
    out_specs = [row(1024), row(512), row(512), row(512), row(512), row(LANES), row(IDX_DIM)]
    if transposed:
        out_shape += [jax.ShapeDtypeStruct((M // tck, 512, tck), BF16),
                      jax.ShapeDtypeStruct((512, M), BF16), jax.ShapeDtypeStruct((512, M), BF16),
                      jax.ShapeDtypeStruct((M, 256), BF16), jax.ShapeDtypeStruct((IDX_HEADS, M), F32)]
        out_specs += [pl.BlockSpec((tm // tck, 512, tck), lambda i: (i, 0, 0)),
                      pl.BlockSpec((512, tm), lambda i: (0, i)), pl.BlockSpec((512, tm), lambda i: (0, i)),
                      row(256), pl.BlockSpec((IDX_HEADS, tm), lambda i: (0, i))]
    return pl.pallas_call(
        functools.partial(_prep_kernel, tck=tck),
        out_shape=out_shape, grid=(nt,), in_specs=in_specs, out_specs=out_specs,
        compiler_params=_cp("parallel"), name="dsa_prep",
    )(z, z, z, z, z, *tabs_a, *tabs_i, qg, kg, ikg)


def _ret_kernel(q_ref, k_ref, v_ref, rg_ref, cos_ref, sin_ref, lg_ref, o_ref, st_ref, s_ref, *, nb):
    n = pl.program_id(0)
    C = RET_CHUNK

    @pl.when(n == 0)
    def _():
        s_ref[...] = jnp.zeros_like(s_ref)

    cos, sin = cos_ref[...], sin_ref[...]
    ri = lax.broadcasted_iota(I32, (C, C), 0).astype(F32)
    ci = lax.broadcasted_iota(I32, (C, C), 1).astype(F32)
    diff = ri - ci
    rv = lax.broadcasted_iota(I32, (C, RET_DV), 0).astype(F32)
    for h in range(RET_HEADS):
        lg = lg_ref[h:h + 1, :]
        lg2 = jnp.concatenate([lg, lg], axis=1)
        dmask = jnp.where(diff >= 0, jnp.exp(jnp.maximum(diff, 0.0) * lg), 0.0)
        cross = jnp.exp((rv + 1.0) * lg2)
        kdec = jnp.exp((C - 1.0 - ri) * lg)
        g_c = jnp.exp(C * lg2)
        for b in range(nb):
            qs = slice(h * RET_DK, (h + 1) * RET_DK)
            vs = slice(h * RET_DV, (h + 1) * RET_DV)
            q = q_ref[b, :, qs]
            k = k_ref[b, :, qs]
            v = v_ref[b, :, vs]
            qr = q * cos + pltpu.roll(q, RET_DK // 2, 1) * sin
            kr = (k * cos + pltpu.roll(k, RET_DK // 2, 1) * sin) * (RET_DK ** -0.5)
            s_old = s_ref[b * RET_HEADS + h]
            sc = _dot3(qr, kr, nt=True) * dmask
            out = _dot3(sc, v) + _dot3(qr, s_old) * cross
            s_ref[b * RET_HEADS + h] = g_c * s_old + _dot3((kr * kdec).T, v)
            rg = rg_ref[b, :, vs]
            o_ref[b, :, vs] = (_silu(rg) * _rms(out)).astype(BF16)

    @pl.when(n == pl.num_programs(0) - 1)
    def _():
        for b in range(nb):
            for h in range(RET_HEADS):
                st_ref[b, h] = s_ref[b * RET_HEADS + h]


def _retention_prompt(z3, cos_r, sin_r, lg):
    B, T, _ = z3.shape
    C = RET_CHUNK
    return pl.pallas_call(
        functools.partial(_ret_kernel, nb=B),
        out_shape=[jax.ShapeDtypeStruct((B, T, 1024), BF16),
                   jax.ShapeDtypeStruct((B, RET_HEADS, RET_DK, RET_DV), F32)],
        grid=(T // C,),
        in_specs=[pl.BlockSpec((B, C, 512), lambda n: (0, n, Z_RQ // 512)),
                  pl.BlockSpec((B, C, 512), lambda n: (0, n, Z_RK // 512)),
                  pl.BlockSpec((B, C, 1024), lambda n: (0, n, Z_RV // 1024)),
                  pl.BlockSpec((B, C, 1024), lambda n: (0, n, Z_RG // 1024)),
                  pl.BlockSpec((C, LANES), lambda n: (n, 0)),
                  pl.BlockSpec((C, LANES), lambda n: (n, 0)),
                  pl.BlockSpec((RET_HEADS, LANES), lambda n: (0, 0))],
        out_specs=[pl.BlockSpec((B, C, 1024), lambda n: (0, n, 0)),
                   pl.BlockSpec((B, RET_HEADS, RET_DK, RET_DV), lambda n: (0, 0, 0, 0))],
        scratch_shapes=[pltpu.VMEM((B * RET_HEADS, RET_DK, RET_DV), F32)],
        compiler_params=_cp("arbitrary"),
        name="retention_prompt",
    )(z3, z3, z3, z3, cos_r, sin_r, lg)


def _ret_step_kernel(q_ref, k_ref, v_ref, rg_ref, s0_ref, cos_ref, sin_ref, lg_ref, o_ref, st_ref):
    cos, sin = cos_ref[...], sin_ref[...]
    eye = (lax.broadcasted_iota(I32, (RET_DK, RET_DK), 0) == lax.broadcasted_iota(I32, (RET_DK, RET_DK), 1))
    for h in range(RET_HEADS):
        qs = slice(h * RET_DK, (h + 1) * RET_DK)
        vs = slice(h * RET_DV, (h + 1) * RET_DV)
        q, k, v = q_ref[:, qs], k_ref[:, qs], v_ref[:, vs]
        qr = q * cos + pltpu.roll(q, RET_DK // 2, 1) * sin
        kr = (k * cos + pltpu.roll(k, RET_DK // 2, 1) * sin) * (RET_DK ** -0.5)
        lg = lg_ref[h:h + 1, :]
        gamma = jnp.exp(jnp.concatenate([lg, lg], axis=1))
        qcol = jnp.sum(jnp.where(eye, qr, 0.0), axis=1, keepdims=True)
        kcol = jnp.sum(jnp.where(eye, kr, 0.0), axis=1, keepdims=True)
        s0 = s0_ref[h]
        qk = jnp.sum(qr * kr, axis=1, keepdims=True)
        out = qk * v + jnp.sum(qcol * s0, axis=0, keepdims=True) * gamma
        st_ref[h] = gamma * s0 + kcol * v
        o_ref[:, vs] = (_silu(rg_ref[:, vs]) * _rms(out)).astype(BF16)


def _retention_sample(z3, state_ret, layer, cos_r, sin_r, lg):
    Bs = z3.shape[0]
    zc = lambda w, off: pl.BlockSpec((None, 1, w), lambda b, o=off // w: (b, 0, o))
    return pl.pallas_call(
        _ret_step_kernel,
        out_shape=[jax.ShapeDtypeStruct((Bs, 1, 1024), BF16),
                   jax.ShapeDtypeStruct((Bs, RET_HEADS, RET_DK, RET_DV), F32)],
        grid=(Bs,),
        in_specs=[zc(512, Z_RQ), zc(512, Z_RK), zc(1024, Z_RV), zc(1024, Z_RG),
                  pl.BlockSpec((None, None, RET_HEADS, RET_DK, RET_DV), lambda b: (layer, b, 0, 0, 0)),
                  pl.BlockSpec((1, LANES), lambda b: (0, 0)),
                  pl.BlockSpec((1, LANES), lambda b: (0, 0)),
                  pl.BlockSpec((RET_HEADS, LANES), lambda b: (0, 0))],
        out_specs=[pl.BlockSpec((None, 1, 1024), lambda b: (b, 0, 0)),
                   pl.BlockSpec((None, RET_HEADS, RET_DK, RET_DV), lambda b: (b, 0, 0, 0))],
        compiler_params=_cp("parallel"),
        name="retention_sample",
    )(z3, z3, z3, z3, state_ret, cos_r, sin_r, lg)


CONV_HALO = 32
CONV_RB = 32
CONV_CB = 512


def _conv_kernel(a_ref, g_ref, w_ref, b_ref, cg_ref, o_ref, cs_ref, u_ref, y_ref):
    t = pl.program_id(1)
    tm = a_ref.shape[0]

    @pl.when(t == 0)
    def _():
        u_ref[0:CONV_HALO, :] = jnp.zeros((CONV_HALO, u_ref.shape[1]), F32)

    u_ref[CONV_HALO:CONV_HALO + tm, :] = a_ref[...] * jax.nn.sigmoid(g_ref[...])
    first = CONV_HALO - (CONV_W - 1)
    for r in range(tm // CONV_RB):
        for c in range(u_ref.shape[1] // CONV_CB):
            cs = slice(c * CONV_CB, (c + 1) * CONV_CB)
            acc = jnp.zeros((CONV_RB, CONV_CB), F32)
            for j in range(CONV_W):
                acc = acc + w_ref[j:j + 1, cs] * u_ref[pl.ds(r * CONV_RB + first + j, CONV_RB), cs]
            y_ref[r * CONV_RB:(r + 1) * CONV_RB, cs] = acc
    cy = y_ref[...] + b_ref[...]
    o_ref[...] = _silu(_rms(cy) * cg_ref[...]).astype(BF16)

    @pl.when(t == pl.num_programs(1) - 1)
    def _():
        cs_ref[...] = u_ref[tm + first:tm + CONV_HALO, :]

    u_ref[0:CONV_HALO, :] = u_ref[tm:tm + CONV_HALO, :]


def _conv_prompt(z, B, T, conv_w, conv_b, conv_g, layer, tm):
    M, C = B * T, 1024
    nt = T // tm
    return pl.pallas_call(
        _conv_kernel,
        out_shape=[jax.ShapeDtypeStruct((M, C), BF16), jax.ShapeDtypeStruct((B, CONV_W - 1, C), F32)],
        grid=(B, nt),
        in_specs=[pl.BlockSpec((tm, C), lambda b, t: (b * nt + t, Z_CU // C)),
                  pl.BlockSpec((tm, C), lambda b, t: (b * nt + t, Z_CG // C)),
                  pl.BlockSpec((None, CONV_W, C), lambda b, t: (layer, 0, 0)),
                  pl.BlockSpec((None, 1, C), lambda b, t: (layer, 0, 0)),
                  pl.BlockSpec((None, 1, C), lambda b, t: (layer, 0, 0))],
        out_specs=[pl.BlockSpec((tm, C), lambda b, t: (b * nt + t, 0)),
                   pl.BlockSpec((None, CONV_W - 1, C), lambda b, t: (b, 0, 0))],
        scratch_shapes=[pltpu.VMEM((tm + CONV_HALO, C), F32), pltpu.VMEM((tm, C), F32)],
        compiler_params=_cp("parallel", "arbitrary"),
        name="conv_prompt",
    )(z, z, conv_w, conv_b, conv_g)


def _conv_step_kernel(a_ref, g_ref, buf_ref, w_ref, b_ref, cg_ref, o_ref, cs_ref):
    u = a_ref[...] * jax.nn.sigmoid(g_ref[...])
    buf = buf_ref[...]
    y = jnp.sum(w_ref[0:CONV_W - 1, :] * buf, axis=0, keepdims=True) + w_ref[CONV_W - 1:CONV_W, :] * u
    cy = y + b_ref[...]
    o_ref[...] = _silu(_rms(cy) * cg_ref[...]).astype(BF16)
    cs_ref[0:CONV_W - 2, :] = buf[1:CONV_W - 1, :]
    cs_ref[CONV_W - 2:CONV_W - 1, :] = u


def _conv_sample(z3, state_conv, conv_w, conv_b, conv_g, layer):
    Bs, C = z3.shape[0], 1024
    return pl.pallas_call(
        _conv_step_kernel,
        out_shape=[jax.ShapeDtypeStruct((Bs, 1, C), BF16), jax.ShapeDtypeStruct((Bs, CONV_W - 1, C), F32)],
        grid=(Bs,),
        in_specs=[pl.BlockSpec((None, 1, C), lambda b: (b, 0, Z_CU // C)),
                  pl.BlockSpec((None, 1, C), lambda b: (b, 0, Z_CG // C)),
                  pl.BlockSpec((None, None, CONV_W - 1, C), lambda b: (layer, b, 0, 0)),
                  pl.BlockSpec((None, CONV_W, C), lambda b: (layer, 0, 0)),
                  pl.BlockSpec((None, 1, C), lambda b: (layer, 0, 0)),
                  pl.BlockSpec((None, 1, C), lambda b: (layer, 0, 0))],
        out_specs=[pl.BlockSpec((None, 1, C), lambda b: (b, 0, 0)),
                   pl.BlockSpec((None, CONV_W - 1, C), lambda b: (b, 0, 0))],
        compiler_params=_cp("parallel"),
        name="conv_sample",
    )(z3, z3, state_conv, conv_w, conv_b, conv_g)


def _sel_kernel(ik3_ref, iqh_ref, iql_ref, iw_ref, m_ref, key_ref, w_ref, j_ref, *, T, K, ck):
    i = pl.program_id(1)
    nq = LANES
    for h in range(IDX_HEADS):
        rs = slice(h * IDX_DIM, (h + 1) * IDX_DIM)
        cs = slice(h * nq, (h + 1) * nq)
        hi = iqh_ref[rs, :]
        w_ref[0:IDX_DIM, cs] = hi
        w_ref[IDX_DIM:2 * IDX_DIM, cs] = hi
        w_ref[2 * IDX_DIM:3 * IDX_DIM, cs] = iql_ref[rs, :]
        w_ref[3 * IDX_DIM:4 * IDX_DIM, cs] = jnp.zeros((IDX_DIM, nq), BF16)
    nch = ((i + 1) * nq + ck - 1) // ck
    tq = i * nq + lax.broadcasted_iota(I32, (ck, nq), 1)
    row = lax.broadcasted_iota(I32, (ck, nq), 0)
    iw = iw_ref[...]

    def score_body(c, carry):
        off = pl.multiple_of(c * ck, ck)
        d = _dot(ik3_ref[pl.ds(off, ck), :], w_ref[...])
        s = jnp.zeros((ck, nq), F32)
        for h in range(IDX_HEADS):
            s = s + jnp.maximum(d[:, h * nq:(h + 1) * nq], 0.0) * iw[h:h + 1, :]
        key_ref[pl.ds(off, ck), :] = jnp.where(off + row <= tq, _sortable(s), INT_MIN)
        return carry

    lax.fori_loop(0, nch, score_body, 0)

    def count(pred):
        def body(c, acc):
            off = pl.multiple_of(c * ck, ck)
            hit = jnp.where(pred(key_ref[pl.ds(off, ck), :], off), 1.0, 0.0)
            return acc + jnp.sum(hit.reshape(ck // COUNT_ROWS, COUNT_ROWS, nq), axis=0)
        acc = lax.fori_loop(0, nch, body, jnp.zeros((COUNT_ROWS, nq), F32))
        return jnp.sum(acc, axis=0, keepdims=True)

    tau = jnp.full((1, nq), INT_MIN, I32)
    for bit in range(31, -1, -1):
        cand = jnp.zeros((1, nq), I32) if bit == 31 else tau | (1 << bit)
        cnt = count(lambda kk, off, cand=cand: kk >= cand)
        tau = jnp.where(cnt >= K, cand, tau)
    tau = jnp.maximum(tau, INT_MIN + 1)
    n_gt = count(lambda kk, off: kk > tau)
    n_ge = count(lambda kk, off: kk >= tau)
    need = K - n_gt
    j_ref[...] = jnp.full((1, nq), 2 ** 30, I32)

    @pl.when(jnp.max(n_ge) > K)
    def _():
        jj = jnp.zeros((1, nq), I32)
        for bit in range(T.bit_length() - 1, -1, -1):
            cand = jj | (1 << bit)
            f = count(lambda kk, off, cand=cand: jnp.where(kk == tau, off + row, 2 ** 30) < cand)
            jj = jnp.where(f <= need, cand, jj)
        j_ref[...] = jj

    jlim = j_ref[...]

    def write_body(c, carry):
        off = pl.multiple_of(c * ck, ck)
        kk = key_ref[pl.ds(off, ck), :]
        tie_pos = jnp.where(kk == tau, off + row, 2 ** 30)
        bias = jnp.where(kk > tau, 0.0, jnp.where(tie_pos < jlim, 0.0, MASK_BIAS))
        m_ref[pl.ds(off, ck), :] = bias.astype(BF16)
        return carry

    lax.fori_loop(0, nch, write_body, 0)

    def fill_body(c, carry):
        off = pl.multiple_of(c * ck, ck)
        m_ref[pl.ds(off, ck), :] = jnp.full((ck, nq), MASK_BIAS, BF16)
        return carry

    lax.fori_loop(nch, T // ck, fill_body, 0)


def _select_prompt(ik3, iqth, iqtl, iwt, B, T, ck):
    nq = T // LANES
    K = min(TOPK_MAX, T // 4)
    return pl.pallas_call(
        functools.partial(_sel_kernel, T=T, K=K, ck=ck),
        out_shape=jax.ShapeDtypeStruct((B, T, T), BF16),
        grid=(B, nq),
        in_specs=[pl.BlockSpec((T, 256), lambda b, i: (b, 0)),
                  pl.BlockSpec((512, LANES), lambda b, i: (0, b * nq + i)),
                  pl.BlockSpec((512, LANES), lambda b, i: (0, b * nq + i)),
                  pl.BlockSpec((IDX_HEADS, LANES), lambda b, i: (0, b * nq + i))],
        out_specs=pl.BlockSpec((None, T, LANES), lambda b, i: (b, 0, i)),
        scratch_shapes=[pltpu.VMEM((T, LANES), I32), pltpu.VMEM((256, IDX_HEADS * LANES), BF16),
                        pltpu.VMEM((1, LANES), I32)],
        compiler_params=_cp("parallel", "arbitrary"),
        name="dsa_select_prompt",
    )(ik3, iqth, iqtl, iwt)


def _att_kernel(q_ref, k_ref, vt_ref, m_ref, o_ref, *, ck, big):
    i = pl.program_id(2)
    nq = LANES
    n_big = ((i + 1) * nq) // big
    n_tail = ((i + 1) * nq - n_big * big + ck - 1) // ck
    q2 = q_ref[...]
    qs = jnp.concatenate([q2[:, :HEAD_DIM], q2[:, HEAD_DIM:]], axis=0)

    def make_body(rows, base):
        def body(c, carry):
            m, l, acc = carry
            off = pl.multiple_of(base + c * rows, ck)
            st = _dot_nt(k_ref[pl.ds(off, rows), :], qs)
            bias = m_ref[pl.ds(off, rows), :].astype(F32)
            sm = st + jnp.concatenate([bias, bias], axis=1)
            m_new = jnp.maximum(m, jnp.max(sm, axis=0, keepdims=True))
            alpha = jnp.exp(m - m_new)
            p = jnp.exp(sm - m_new)
            l = l * alpha + jnp.sum(p, axis=0, keepdims=True)
            pb = p.astype(BF16)
            acc = acc * alpha
            for u in range(rows // ck):
                acc = acc + _dot(vt_ref[off // ck + u], pb[u * ck:(u + 1) * ck, :])
            return m_new, l, acc
        return body

    init = (jnp.full((1, 2 * nq), M_INIT, F32), jnp.zeros((1, 2 * nq), F32), jnp.zeros((HEAD_DIM, 2 * nq), F32))
    carry = lax.fori_loop(0, n_big, make_body(big, 0), init)
    _, l, acc = lax.fori_loop(0, n_tail, make_body(ck, n_big * big), carry)
    ot = acc / l
    o_ref[:, 0:HEAD_DIM] = ot[:, 0:nq].T.astype(BF16)
    o_ref[:, HEAD_DIM:2 * HEAD_DIM] = ot[:, nq:2 * nq].T.astype(BF16)


def _attend_prompt(qa, kb, vt3, mask, B, T, ck):
    nq = T // LANES
    M = B * T
    return pl.pallas_call(
        functools.partial(_att_kernel, ck=ck, big=min(4 * ck, T)),
        out_shape=jax.ShapeDtypeStruct((M, ATT_HEADS * HEAD_DIM), BF16),
        grid=(B, KV_HEADS, nq),
        in_specs=[pl.BlockSpec((LANES, 2 * HEAD_DIM), lambda b, g, i: (b * nq + i, g)),
                  pl.BlockSpec((T, HEAD_DIM), lambda b, g, i: (b, g)),
                  pl.BlockSpec((T // ck, HEAD_DIM, ck), lambda b, g, i: (b, g, 0)),
                  pl.BlockSpec((None, T, LANES), lambda b, g, i: (b, 0, i))],
        out_specs=pl.BlockSpec((LANES, 2 * HEAD_DIM), lambda b, g, i: (b * nq + i, g)),
        compiler_params=_cp("parallel", "parallel", "arbitrary"),
        name="dsa_attend_prompt",
    )(qa, kb, vt3, mask)


def _sel_step_kernel(pt_ref, iq_ref, iw_ref, iko_ref, *refs, pps, K, n_pages):
    ik_refs = refs[:pps]
    msk_ref, own_ref, sc_ref = refs[pps:]
    p = pl.program_id(1)
    iq = iq_ref[...]
    iw = iw_ref[...]
    for j in range(pps):
        d = _dot3(iq, ik_refs[j][...])
        sc_ref[pl.ds(p * pps + j, 1), :] = jnp.sum(jnp.maximum(d, 0.0) * iw, axis=0, keepdims=True)

    @pl.when(p == pl.num_programs(1) - 1)
    def _():
        key = _sortable(sc_ref[...])
        d_own = jnp.sum(iq * iko_ref[...], axis=1, keepdims=True)
        s_own = jnp.sum(jnp.maximum(d_own, 0.0) * iw, axis=0, keepdims=True)
        key_own = _sortable(s_own)
        pos = (lax.broadcasted_iota(I32, key.shape, 0) * PAGE + lax.broadcasted_iota(I32, key.shape, 1))
        pos_own = n_pages * PAGE

        def count(pred):
            c = jnp.sum(jnp.where(pred(key, pos), 1.0, 0.0), axis=1, keepdims=True)
            c = jnp.sum(c, axis=0, keepdims=True)
            return c + jnp.where(pred(key_own, pos_own), 1.0, 0.0)

        tau = jnp.full((1, 1), INT_MIN, I32)
        for bit in range(31, -1, -1):
            cand = jnp.zeros((1, 1), I32) if bit == 31 else tau | (1 << bit)
            tau = jnp.where(count(lambda kk, ps, cand=cand: kk >= cand) >= K, cand, tau)
        tau = jnp.maximum(tau, INT_MIN + 1)
        need = K - count(lambda kk, ps: kk > tau)
        jj = jnp.zeros((1, 1), I32)
        for bit in range((pos_own + 1).bit_length() - 1, -1, -1):
            cand = jj | (1 << bit)
            f = count(lambda kk, ps, cand=cand: jnp.where(kk == tau, ps, 2 ** 30) < cand)
            jj = jnp.where(f <= need, cand, jj)

        def selected(kk, ps):
            return jnp.where(kk > tau, 1.0, jnp.where(jnp.where(kk == tau, ps, 2 ** 30) < jj, 1.0, 0.0))

        msk_ref[...] = selected(key, pos)
        own_ref[...] = jnp.broadcast_to(selected(key_own, pos_own), own_ref.shape)


def _select_sample(page_table, iq8, iw8, ik_own, cache_idx_t, layer, pps):
    Bs, n_pages = page_table.shape
    K = min(TOPK_MAX, (n_pages * PAGE + 1) // 4)
    page_spec = lambda j: pl.BlockSpec((None, None, IDX_DIM, PAGE),
                                       lambda b, p, pt, j=j: (layer, pt[b, p * pps + j], 0, 0))
    grid_spec = pltpu.PrefetchScalarGridSpec(
        num_scalar_prefetch=1,
        grid=(Bs, n_pages // pps),
        in_specs=[pl.BlockSpec((None, IDX_HEADS, IDX_DIM), lambda b, p, pt: (b, 0, 0)),
                  pl.BlockSpec((None, IDX_HEADS, 1), lambda b, p, pt: (b, 0, 0)),
                  pl.BlockSpec((None, 1, IDX_DIM), lambda b, p, pt: (b, 0, 0))]
                 + [page_spec(j) for j in range(pps)],
        out_specs=[pl.BlockSpec((None, n_pages, PAGE), lambda b, p, pt: (b, 0, 0)),
                   pl.BlockSpec((None, 1, LANES), lambda b, p, pt: (b, 0, 0))],
        scratch_shapes=[pltpu.VMEM((n_pages, PAGE), F32)],
    )
    return pl.pallas_call(
        functools.partial(_sel_step_kernel, pps=pps, K=K, n_pages=n_pages),
        out_shape=[jax.ShapeDtypeStruct((Bs, n_pages, PAGE), F32), jax.ShapeDtypeStruct((Bs, 1, LANES), F32)],
        grid_spec=grid_spec,
        compiler_params=_cp("parallel", "arbitrary"),
        name="dsa_select_sample",
    )(page_table, iq8, iw8, ik_own, *([cache_idx_t] * pps))


def _att_step_kernel(pt_ref, q_ref, msk_ref, own_ref, ko_ref, vo_ref, *refs, pp):
    k_refs, v_refs = refs[:pp], refs[pp:2 * pp]
    o_ref, m_sc, l_sc, acc_sc = refs[2 * pp:]
    p = pl.program_id(1)
    rows = PAGE * KV_HEADS
    grp = ATT_HEADS // KV_HEADS

    @pl.when(p == 0)
    def _():
        m_sc[...] = jnp.full(m_sc.shape, NEG, F32)
        l_sc[...] = jnp.zeros(l_sc.shape, F32)
        acc_sc[...] = jnp.zeros(acc_sc.shape, F32)

    q8 = q_ref[...]
    own_kv = (lax.broadcasted_iota(I32, (ATT_HEADS, rows), 1) % KV_HEADS
              == lax.broadcasted_iota(I32, (ATT_HEADS, rows), 0) // grp)
    expand = jnp.where(lax.broadcasted_iota(I32, (PAGE, rows), 1) // KV_HEADS
                       == lax.broadcasted_iota(I32, (PAGE, rows), 0), 1.0, 0.0).astype(BF16)
    sms = []
    for j in range(pp):
        s = _dot_nt(q8, k_refs[j][...].astype(BF16))
        mrow = jnp.broadcast_to(msk_ref[pl.ds(p * pp + j, 1), :], (ATT_HEADS, PAGE)).astype(BF16)
        picked = _dot(mrow, expand)
        sms.append(jnp.where(jnp.where(own_kv, picked, 0.0) > 0.0, s, NEG))
    m = m_sc[...]
    m_new = m
    for sm in sms:
        m_new = jnp.maximum(m_new, jnp.max(sm, axis=1, keepdims=True))
    alpha = jnp.exp(m - m_new)
    l = l_sc[...] * alpha
    acc = acc_sc[...] * alpha
    for j in range(pp):
        pr = jnp.where(sms[j] > 0.5 * NEG, jnp.exp(sms[j] - m_new), 0.0)
        l = l + jnp.sum(pr, axis=1, keepdims=True)
        acc = acc + _dot(pr.astype(BF16), v_refs[j][...].astype(BF16))
    m_sc[...], l_sc[...], acc_sc[...] = m_new, l, acc

    @pl.when(p == pl.num_programs(1) - 1)
    def _():
        ko = ko_ref[...].astype(BF16).astype(F32)
        s_own = jnp.sum(q8.astype(F32) * ko, axis=1, keepdims=True)
        sel_own = own_ref[:, 0:1] > 0.0
        sm = jnp.where(sel_own, s_own, NEG)
        m_fin = jnp.maximum(m_new, sm)
        a_fin = jnp.exp(m_new - m_fin)
        pr = jnp.where(sel_own, jnp.exp(sm - m_fin), 0.0)
        o = (acc * a_fin + pr * vo_ref[...]) / (l * a_fin + pr)
        o_ref[...] = o.astype(BF16)


def _attend_sample(page_table, q8, msk, own, k_own8, v_own8, cache_k, cache_v, layer, pp):
    Bs, n_pages = page_table.shape
    rows = PAGE * KV_HEADS
    page_spec = lambda j: pl.BlockSpec((None, None, rows, HEAD_DIM),
                                       lambda b, p, pt, j=j: (layer, pt[b, p * pp + j], 0, 0))
    per_b = lambda r, w: pl.BlockSpec((None, r, w), lambda b, p, pt: (b, 0, 0))
    grid_spec = pltpu.PrefetchScalarGridSpec(
        num_scalar_prefetch=1,
        grid=(Bs, n_pages // pp),
        in_specs=[per_b(ATT_HEADS, HEAD_DIM), per_b(n_pages, PAGE), per_b(1, LANES),
                  per_b(ATT_HEADS, HEAD_DIM), per_b(ATT_HEADS, HEAD_DIM)]
                 + [page_spec(j) for j in range(pp)] + [page_spec(j) for j in range(pp)],
        out_specs=per_b(ATT_HEADS, HEAD_DIM),
        scratch_shapes=[pltpu.VMEM((ATT_HEADS, 1), F32), pltpu.VMEM((ATT_HEADS, 1), F32),
                        pltpu.VMEM((ATT_HEADS, HEAD_DIM), F32)],
    )
    return pl.pallas_call(
        functools.partial(_att_step_kernel, pp=pp),
        out_shape=jax.ShapeDtypeStruct((Bs, ATT_HEADS, HEAD_DIM), BF16),
        grid_spec=grid_spec,
        compiler_params=_cp("parallel", "arbitrary"),
        name="dsa_attend_sample",
    )(page_table, q8, msk, own, k_own8, v_own8, *([cache_k] * pp), *([cache_v] * pp))


def _mixout_kernel(x_ref, ro_ref, ca_ref, ao_ref, g0_ref, g1_ref, g2_ref, gm_ref,
                   wr_ref, wc_ref, wa_ref, wo_ref, o_ref):
    merged = (jax.nn.sigmoid(g0_ref[...]) * _dot(ro_ref[...], wr_ref[...])
              + jax.nn.sigmoid(g1_ref[...]) * _dot(ca_ref[...], wc_ref[...])
              + jax.nn.sigmoid(g2_ref[...]) * _dot(ao_ref[...], wa_ref[...]))
    o_ref[...] = x_ref[...] + gm_ref[0] * _dot(merged.astype(BF16), wo_ref[...])


def _mixout(x, ro, ca, ao, z, gate, w_r, w_c, w_a, w_o, tm, rpg):
    M, D = x.shape
    row = pl.BlockSpec((tm, D), lambda i: (i, 0))
    glc = lambda k: pl.BlockSpec((tm, D), lambda i, o=Z_GL // D + k: (i, o))
    wsp = pl.BlockSpec((D, D), lambda i: (0, 0))
    return pl.pallas_call(
        _mixout_kernel,
        out_shape=jax.ShapeDtypeStruct((M, D), F32),
        grid=(M // tm,),
        in_specs=[row, row, row, row, glc(0), glc(1), glc(2), _mod_spec(gate, tm, rpg), wsp, wsp, wsp, wsp],
        out_specs=row,
        compiler_params=_cp("parallel"),
        name="mixer_out",
    )(x, ro, ca, ao, z, z, z, gate, w_r, w_c, w_a, w_o)


def _rope_tables(pos, rot_dim, theta, period):
    half = rot_dim // 2
    inv = 1.0 / (theta ** (jnp.arange(half, dtype=F32) / half))
    ang = pos.astype(F32)[:, None] * inv[None, :]
    lane = np.arange(LANES) % period
    idx = lane % half
    cos = jnp.where(lane < rot_dim, jnp.cos(ang)[:, idx], 1.0)
    sin = jnp.sin(ang)[:, idx]
    s_lo = jnp.where(lane < half, -sin, 0.0)
    s_hi = jnp.where((lane >= half) & (lane < rot_dim), sin, 0.0)
    return cos, s_lo, s_hi


def _tables(pos):
    ca = _rope_tables(pos, ROT_DIM, ROPE_THETA, HEAD_DIM)
    ci = _rope_tables(pos, IDX_ROT, ROPE_THETA, IDX_DIM)
    cr, r_lo, r_hi = _rope_tables(pos, RET_DK, RET_THETA, RET_DK)
    return ca, ci, (cr, r_lo + r_hi)


def _mods(mod, rows):
    parts = jnp.split(mod, 9, axis=-1)
    if rows == 1:
        return [p[:, None, :] for p in parts]
    return [p[None] for p in parts]


def _cat_w_in(w):
    pad = Z_END - Z_IX - (IDX_DIM + IDX_HEADS)
    return jnp.concatenate([w[:, :6144], w[:, 7752:10824], w[:, 6144:7680], w[:, 7680:7752],
                            jnp.zeros((w.shape[0], pad), w.dtype)], axis=1).astype(BF16)


def kernel(x_prompt, x_sample, cache_k, cache_v, cache_idx_k, state_ret, state_conv, page_table, c_prompt, c_sample, w_ada, b_ada, norm_g, w_ff1_up, w_ff1_down, w_ff2_up, w_ff2_down, w_in, conv_w, conv_b, conv_g, q_norm_g, k_norm_g, idx_k_norm_g, w_ret_o, w_conv_o, w_att_o, w_o):
    B, T, D = x_prompt.shape
    Bs, Ts, _ = x_sample.shape
    assert Ts == 1 and T % 512 == 0 and D == 1024
    depth = w_ada.shape[0]
    n_pages = page_table.shape[1]
    past = n_pages * PAGE
    tm_p = 512
    ck = 256
    pps = 16 if n_pages % 16 == 0 else 4
    pp = 8 if n_pages % 8 == 0 else 4
    grp = ATT_HEADS // KV_HEADS

    tabs_p = _tables(jnp.arange(T, dtype=jnp.int32))
    tabs_s = _tables(jnp.full((Bs,), past, dtype=jnp.int32))
    lg = jnp.broadcast_to(jnp.log(1.0 - 2.0 ** (-5.0 - jnp.arange(RET_HEADS, dtype=F32)))[:, None],
                          (RET_HEADS, LANES))
    c_all = jnp.concatenate([c_prompt, c_sample], axis=0)
    ck4 = cache_k.reshape(cache_k.shape[0], cache_k.shape[1], PAGE * KV_HEADS, HEAD_DIM)
    cv4 = cache_v.reshape(cache_v.shape[0], cache_v.shape[1], PAGE * KV_HEADS, HEAD_DIM)
    cache_idx_t = jnp.swapaxes(cache_idx_k, 2, 3)
    conv_b3 = conv_b.reshape(depth, 1, D)
    conv_g3 = conv_g.reshape(depth, 1, D)

    xp = x_prompt.reshape(B * T, D)
    xs = x_sample.reshape(Bs, D)
    outs_p, outs_s = [], []
    for l in range(depth):
        mod = _ada_mod(c_all, w_ada, b_ada, l)
        mp = _mods(mod[:B], 1)
        msm = _mods(mod[B:], Bs)
        wu1, wd1 = w_ff1_up[l].astype(BF16), w_ff1_down[l].astype(BF16)
        wu2, wd2 = w_ff2_up[l].astype(BF16), w_ff2_down[l].astype(BF16)
        w_cat = _cat_w_in(w_in[l])
        w_r, w_c = w_ret_o[l].astype(BF16), w_conv_o[l].astype(BF16)
        w_a, w_oo = w_att_o[l].astype(BF16), w_o[l].astype(BF16)
        ng = norm_g[l]
        qg, kg = q_norm_g[l][None, :], k_norm_g[l][None, :]
        ikg = jnp.concatenate([idx_k_norm_g[l], jnp.zeros((LANES - IDX_DIM,), F32)])[None, :]

        xp = _ffn(xp, mp[0], mp[1], mp[2], ng[0:1], wu1, wd1, tm_p, T)
        z = _inproj(xp, mp[3], mp[4], ng[1:2], w_cat, min(1024, T), T)
        (qa, k_p, kb, v_p, _iqr, _ixo, idxk_p, vt3, iqth, iqtl, ik3, iwt) = _prep(
            z, tabs_p[0], tabs_p[1], qg, kg, ikg, tm_p, True, ck)
        ro, ret_p = _retention_prompt(z.reshape(B, T, Z_END), tabs_p[2][0], tabs_p[2][1], lg)
        ca, conv_p = _conv_prompt(z, B, T, conv_w, conv_b3, conv_g3, l, tm_p)
        mask = _select_prompt(ik3, iqth, iqtl, iwt, B, T, 2 * ck)
        ao = _attend_prompt(qa, kb, vt3, mask, B, T, ck)
        xp = _mixout(xp, ro.reshape(B * T, D), ca, ao, z, mp[5], w_r, w_c, w_a, w_oo, tm_p, T)
        xp = _ffn(xp, mp[6], mp[7], mp[8], ng[2:3], wu2, wd2, tm_p, T)
        outs_p.append((k_p.reshape(B, T, KV_HEADS, HEAD_DIM), v_p.reshape(B, T, KV_HEADS, HEAD_DIM),
                       idxk_p.reshape(B, T, IDX_DIM), ret_p, conv_p))

        xs = _ffn(xs, msm[0], msm[1], msm[2], ng[0:1], wu1, wd1, Bs, Bs)
        zs = _inproj(xs, msm[3], msm[4], ng[1:2], w_cat, Bs, Bs)
        qa_s, k_s, _kb, v_s, iqr_s, ixo_s, idxk_s = _prep(
            zs, tabs_s[0], tabs_s[1], qg, kg, ikg, Bs, False, ck)
        zs3 = zs.reshape(Bs, 1, Z_END)
        ro_s, ret_s = _retention_sample(zs3, state_ret, l, tabs_s[2][0][0:1], tabs_s[2][1][0:1], lg)
        ca_s, conv_s = _conv_sample(zs3, state_conv, conv_w, conv_b3, conv_g3, l)
        iq8 = iqr_s.reshape(Bs, IDX_HEADS, IDX_DIM)
        iw8 = ixo_s[:, IDX_DIM:IDX_DIM + IDX_HEADS].reshape(Bs, IDX_HEADS, 1)
        msk, own = _select_sample(page_table, iq8, iw8, idxk_s.reshape(Bs, 1, IDX_DIM), cache_idx_t, l, pps)
        k_own8 = jnp.repeat(k_s.reshape(Bs, KV_HEADS, HEAD_DIM), grp, axis=1)
        v_own8 = jnp.repeat(v_s.reshape(Bs, KV_HEADS, HEAD_DIM), grp, axis=1)
        ao_s = _attend_sample(page_table, qa_s.reshape(Bs, ATT_HEADS, HEAD_DIM), msk, own,
                              k_own8, v_own8, ck4, cv4, l, pp)
        xs = _mixout(xs, ro_s.reshape(Bs, D), ca_s.reshape(Bs, D), ao_s.reshape(Bs, D), zs, msm[5],
                     w_r, w_c, w_a, w_oo, Bs, Bs)
        xs = _ffn(xs, msm[6], msm[7], msm[8], ng[2:3], wu2, wd2, Bs, Bs)
        outs_s.append((k_s.reshape(Bs, 1, KV_HEADS, HEAD_DIM), v_s.reshape(Bs, 1, KV_HEADS, HEAD_DIM),
                       idxk_s.reshape(Bs, 1, IDX_DIM), ret_s, conv_s))

    stack = lambda outs, j: jnp.stack([o[j] for o in outs])
    return (xp.reshape(B, T, D), xs.reshape(Bs, 1, D),
            stack(outs_p, 0), stack(outs_p, 1), stack(outs_p, 2), stack(outs_p, 3), stack(outs_p, 4),
            stack(outs_s, 0), stack(outs_s, 1), stack(outs_s, 2), stack(outs_s, 3), stack(outs_s, 4))
```

```python
import functools

import numpy as np
import jax
import jax.numpy as jnp
from jax import lax
from jax.experimental import pallas as pl
from jax.experimental.pallas import tpu as pltpu

F32 = jnp.float32
BF16 = jnp.bfloat16
I32 = jnp.int32

EPS = 1e-6
LANES = 128
SUBLANES = 8
NEG = -1e30
MASK_BIAS = -1e30
M_INIT = -1e20
INT_MIN = -2 ** 31
COUNT_ROWS = 64

RET_HEADS = 4
RET_DK = 128
RET_DV = 256
RET_CHUNK = 128
RET_THETA = 10000.0
CONV_W = 31
ATT_HEADS = 8
KV_HEADS = 4
HEAD_DIM = 128
IDX_HEADS = 8
IDX_DIM = 64
TOPK_MAX = 256
ROPE_THETA = 500000.0
ROT_DIM = HEAD_DIM // 4
IDX_ROT = IDX_DIM // 4
PAGE = 128

Z_RQ, Z_RK, Z_RV, Z_RG = 0, 512, 1024, 2048
Z_CU, Z_CG, Z_AQ, Z_GL = 3072, 4096, 5120, 6144
Z_AK, Z_AV, Z_IQ, Z_IX, Z_END = 9216, 9728, 10240, 10752, 10880

VMEM_LIMIT = 56 * 1024 * 1024


def _cp(*sem):
    return pltpu.CompilerParams(dimension_semantics=tuple(sem), vmem_limit_bytes=VMEM_LIMIT)


def _dot(a, b):
    return jnp.dot(a, b, preferred_element_type=F32)


def _dot_nt(a, b):
    return lax.dot_general(a, b, (((1,), (1,)), ((), ())), preferred_element_type=F32)


def _split(x):
    hi = x.astype(BF16)
    lo = (x - hi.astype(F32)).astype(BF16)
    return hi, lo


def _dot3(a, b, nt=False):
    d = _dot_nt if nt else _dot
    ah, al = _split(a)
    bh, bl = _split(b)
    return d(ah, bh) + d(ah, bl) + d(al, bh)


def _silu(x):
    return x * jax.nn.sigmoid(x)


def _rms(x):
    return x * lax.rsqrt(jnp.mean(x * x, axis=-1, keepdims=True) + EPS)


def _norm_mod(x, g, shift, scale):
    return _rms(x) * g * (1.0 + scale) + shift


def _rot(x, cos, s_lo, s_hi, half):
    return x * cos + pltpu.roll(x, LANES - half, 1) * s_lo + pltpu.roll(x, half, 1) * s_hi


def _sortable(s):
    s = jnp.where(s == 0.0, 0.0, s)
    bits = pltpu.bitcast(s, I32)
    return bits ^ ((bits >> 31) & 0x7FFFFFFF)


def _ada_kernel(c_ref, w_ref, b_ref, o_ref):
    o_ref[...] = _dot3(_silu(c_ref[...]), w_ref[...]) + b_ref[...]


def _ada_mod(c, w_ada, b_ada, layer):
    R, D = c.shape
    N = w_ada.shape[2]
    tn = 1024
    return pl.pallas_call(
        _ada_kernel,
        out_shape=jax.ShapeDtypeStruct((R, N), F32),
        grid=(N // tn,),
        in_specs=[pl.BlockSpec((R, D), lambda j: (0, 0)),
                  pl.BlockSpec((None, D, tn), lambda j: (layer, 0, j)),
                  pl.BlockSpec((None, 1, tn), lambda j: (layer, 0, j))],
        out_specs=pl.BlockSpec((R, tn), lambda j: (0, j)),
        compiler_params=_cp("parallel"),
        name="ada_mod",
    )(c, w_ada, b_ada.reshape(b_ada.shape[0], 1, N))


def _ffn_kernel(x_ref, sh_ref, sc_ref, g_ref, ng_ref, wu_ref, wd_ref, o_ref, acc_ref, *, ff, tf):
    x = x_ref[...]
    h = _norm_mod(x, ng_ref[...], sh_ref[0], sc_ref[0]).astype(BF16)
    for j in range(ff // tf):
        a = _dot(h, wu_ref[:, j * tf:(j + 1) * tf])
        b = _dot(h, wu_ref[:, ff + j * tf:ff + (j + 1) * tf])
        act = (_silu(a) * b).astype(BF16)
        upd = _dot(act, wd_ref[j * tf:(j + 1) * tf, :])
        if j == 0:
            acc_ref[...] = upd
        else:
            acc_ref[...] += upd
    o_ref[...] = x + 0.5 * g_ref[0] * acc_ref[...]


def _mod_spec(mod, tm, rpg):
    rm = mod.shape[1]
    return pl.BlockSpec((1, rm, mod.shape[2]), lambda i: ((i * tm) // rpg, 0, 0))


def _ffn(x, shift, scale, gate, ng, w_up, w_down, tm, rpg):
    M, D = x.shape
    ff = w_down.shape[0]
    const = lambda i: (0, 0)
    return pl.pallas_call(
        functools.partial(_ffn_kernel, ff=ff, tf=256),
        out_shape=jax.ShapeDtypeStruct((M, D), F32),
        grid=(M // tm,),
        in_specs=[pl.BlockSpec((tm, D), lambda i: (i, 0)),
                  _mod_spec(shift, tm, rpg), _mod_spec(scale, tm, rpg), _mod_spec(gate, tm, rpg),
                  pl.BlockSpec((1, D), const),
                  pl.BlockSpec((D, 2 * ff), const),
                  pl.BlockSpec((ff, D), const)],
        out_specs=pl.BlockSpec((tm, D), lambda i: (i, 0)),
        scratch_shapes=[pltpu.VMEM((tm, D), F32)],
        compiler_params=_cp("parallel"),
        name="ffn",
    )(x, shift, scale, gate, ng, w_up, w_down)


def _inproj_kernel(x_ref, sh_ref, sc_ref, ng_ref, w_ref, o_ref, h_ref):
    @pl.when(pl.program_id(1) == 0)
    def _():
        h_ref[...] = _norm_mod(x_ref[...], ng_ref[...], sh_ref[0], sc_ref[0]).astype(BF16)

    o_ref[...] = _dot(h_ref[...], w_ref[...])


def _inproj(x, shift, scale, ng, w_cat, tm, rpg):
    M, D = x.shape
    N = w_cat.shape[1]
    tn = N // 5
    ms = lambda mod: pl.BlockSpec((1, mod.shape[1], D), lambda i, j: ((i * tm) // rpg, 0, 0))
    return pl.pallas_call(
        _inproj_kernel,
        out_shape=jax.ShapeDtypeStruct((M, N), F32),
        grid=(M // tm, N // tn),
        in_specs=[pl.BlockSpec((tm, D), lambda i, j: (i, 0)),
                  ms(shift), ms(scale),
                  pl.BlockSpec((1, D), lambda i, j: (0, 0)),
                  pl.BlockSpec((D, tn), lambda i, j: (0, j))],
        out_specs=pl.BlockSpec((tm, tn), lambda i, j: (i, j)),
        scratch_shapes=[pltpu.VMEM((tm, D), BF16)],
        compiler_params=_cp("parallel", "arbitrary"),
        name="inproj",
    )(x, shift, scale, ng, w_cat)


def _prep_kernel(aq_ref, ak_ref, av_ref, iq_ref, ix_ref,
                 ca_ref, sa1_ref, sa2_ref, ci_ref, si1_ref, si2_ref, qg_ref, kg_ref, ikg_ref,
                 qa_ref, k_ref, kb_ref, v_ref, iqr_ref, ixo_ref, idxk_ref, *t_refs, tck):
    ca, sa1, sa2 = ca_ref[...], sa1_ref[...], sa2_ref[...]
    ci, si1, si2 = ci_ref[...], si1_ref[...], si2_ref[...]
    qg, kg = qg_ref[...], kg_ref[...]
    q_scale = HEAD_DIM ** -0.5
    for h in range(ATT_HEADS):
        sl = slice(h * HEAD_DIM, (h + 1) * HEAD_DIM)
        y = _rot(_rms(aq_ref[:, sl]) * qg, ca, sa1, sa2, ROT_DIM // 2)
        qa_ref[:, sl] = (y * q_scale).astype(BF16)
    for h in range(KV_HEADS):
        sl = slice(h * HEAD_DIM, (h + 1) * HEAD_DIM)
        y = _rot(_rms(ak_ref[:, sl]) * kg, ca, sa1, sa2, ROT_DIM // 2)
        k_ref[:, sl] = y
        kb_ref[:, sl] = y.astype(BF16)
    av = av_ref[...]
    v_ref[...] = av
    for j in range(IDX_HEADS * IDX_DIM // LANES):
        sl = slice(j * LANES, (j + 1) * LANES)
        iqr_ref[:, sl] = _rot(iq_ref[:, sl], ci, si1, si2, IDX_ROT // 2)
    blk = ix_ref[...]
    lane = lax.broadcasted_iota(I32, blk.shape, 1)
    ikm = jnp.where(lane < IDX_DIM, blk, 0.0)
    ms = jnp.sum(ikm * ikm, axis=-1, keepdims=True) * (1.0 / IDX_DIM)
    ikr = _rot(ikm * lax.rsqrt(ms + EPS) * ikg_ref[...], ci, si1, si2, IDX_ROT // 2)
    iw_scale = IDX_HEADS ** -0.5 * IDX_DIM ** -0.5
    iws = jnp.where((lane >= IDX_DIM) & (lane < IDX_DIM + IDX_HEADS), blk * iw_scale, 0.0)
    ixo = ikr + iws
    ixo_ref[...] = ixo
    idxk_ref[...] = ikr[:, :IDX_DIM]
    if t_refs:
        vt_ref, iqth_ref, iqtl_ref, ik3_ref, iwt_ref = t_refs
        tm = av.shape[0]
        avt = av.T
        for c in range(tm // tck):
            vt_ref[c] = avt[:, c * tck:(c + 1) * tck].astype(BF16)
        iqt = iqr_ref[...].T
        hi = iqt.astype(BF16)
        iqth_ref[...] = hi
        iqtl_ref[...] = (iqt - hi.astype(F32)).astype(BF16)
        khi = ikr.astype(BF16).astype(F32)
        klo = ikr - khi
        ik3_ref[...] = jnp.concatenate([khi + pltpu.roll(klo, IDX_DIM, 1), khi], axis=1).astype(BF16)
        iwt_ref[...] = ixo.T[IDX_DIM:IDX_DIM + IDX_HEADS, :]


def _prep(z, tabs_a, tabs_i, qg, kg, ikg, tm, transposed, tck):
    M = z.shape[0]
    nt = M // tm
    tpb = tabs_a[0].shape[0] // tm
    tab = pl.BlockSpec((tm, LANES), lambda i: (i % tpb, 0))
    col = lambda w, off: pl.BlockSpec((tm, w), lambda i, o=off // w: (i, o))
    row = lambda w: pl.BlockSpec((tm, w), lambda i: (i, 0))
    vec = pl.BlockSpec((1, LANES), lambda i: (0, 0))
    in_specs = [col(1024, Z_AQ), col(512, Z_AK), col(512, Z_AV), col(512, Z_IQ), col(128, Z_IX)]
    in_specs += [tab] * 6 + [vec] * 3
    out_shape = [jax.ShapeDtypeStruct((M, 1024), BF16), jax.ShapeDtypeStruct((M, 512), F32),
                 jax.ShapeDtypeStruct((M, 512), BF16), jax.ShapeDtypeStruct((M, 512), F32),
                 jax.ShapeDtypeStruct((M, 512), F32), jax.ShapeDtypeStruct((M, LANES), F32),
                 jax.ShapeDtypeStruct((M, IDX_DIM), F32)]
    out_specs = [row(1024), row(512), row(512), row(512), row(512), row(LANES), row(IDX_DIM)]
    if transposed:
        out_shape += [jax.ShapeDtypeStruct((M // tck, 512, tck), BF16),
                      jax.ShapeDtypeStruct((512, M), BF16), jax.ShapeDtypeStruct((512, M), BF16),
                      jax.ShapeDtypeStruct((M, 256), BF16), jax.ShapeDtypeStruct((IDX_HEADS, M), F32)]
        out_specs += [pl.BlockSpec((tm // tck, 512, tck), lambda i: (i, 0, 0)),
                      pl.BlockSpec((512, tm), lambda i: (0, i)), pl.BlockSpec((512, tm), lambda i: (0, i)),
                      row(256), pl.BlockSpec((IDX_HEADS, tm), lambda i: (0, i))]
    return pl.pallas_call(
        functools.partial(_prep_kernel, tck=tck),
        out_shape=out_shape, grid=(nt,), in_specs=in_specs, out_specs=out_specs,
        compiler_params=_cp("parallel"), name="dsa_prep",
    )(z, z, z, z, z, *tabs_a, *tabs_i, qg, kg, ikg)


def _ret_kernel(q_ref, k_ref, v_ref, rg_ref, cos_ref, sin_ref, lg_ref, o_ref, st_ref, s_ref, *, nb):
    n = pl.program_id(0)
    C = RET_CHUNK

    @pl.when(n == 0)
    def _():
        s_ref[...] = jnp.zeros_like(s_ref)

    cos, sin = cos_ref[...], sin_ref[...]
    ri = lax.broadcasted_iota(I32, (C, C), 0).astype(F32)
    ci = lax.broadcasted_iota(I32, (C, C), 1).astype(F32)
    diff = ri - ci
    rv = lax.broadcasted_iota(I32, (C, RET_DV), 0).astype(F32)
    for h in range(RET_HEADS):
        lg = lg_ref[h:h + 1, :]
        lg2 = jnp.concatenate([lg, lg], axis=1)
        dmask = jnp.where(diff >= 0, jnp.exp(jnp.maximum(diff, 0.0) * lg), 0.0)
        cross = jnp.exp((rv + 1.0) * lg2)
        kdec = jnp.exp((C - 1.0 - ri) * lg)
        g_c = jnp.exp(C * lg2)
        for b in range(nb):
            qs = slice(h * RET_DK, (h + 1) * RET_DK)
            vs = slice(h * RET_DV, (h + 1) * RET_DV)
            q = q_ref[b, :, qs]
            k = k_ref[b, :, qs]
            v = v_ref[b, :, vs]
            qr = q * cos + pltpu.roll(q, RET_DK // 2, 1) * sin
            kr = (k * cos + pltpu.roll(k, RET_DK // 2, 1) * sin) * (RET_DK ** -0.5)
            s_old = s_ref[b * RET_HEADS + h]
            sc = _dot3(qr, kr, nt=True) * dmask
            out = _dot3(sc, v) + _dot3(qr, s_old) * cross
            s_ref[b * RET_HEADS + h] = g_c * s_old + _dot3((kr * kdec).T, v)
            rg = rg_ref[b, :, vs]
            o_ref[b, :, vs] = (_silu(rg) * _rms(out)).astype(BF16)

    @pl.when(n == pl.num_programs(0) - 1)
    def _():
        for b in range(nb):
            for h in range(RET_HEADS):
                st_ref[b, h] = s_ref[b * RET_HEADS + h]


def _retention_prompt(z3, cos_r, sin_r, lg):
    B, T, _ = z3.shape
    C = RET_CHUNK
    return pl.pallas_call(
        functools.partial(_ret_kernel, nb=B),
        out_shape=[jax.ShapeDtypeStruct((B, T, 1024), BF16),
                   jax.ShapeDtypeStruct((B, RET_HEADS, RET_DK, RET_DV), F32)],
        grid=(T // C,),
        in_specs=[pl.BlockSpec((B, C, 512), lambda n: (0, n, Z_RQ // 512)),
                  pl.BlockSpec((B, C, 512), lambda n: (0, n, Z_RK // 512)),
                  pl.BlockSpec((B, C, 1024), lambda n: (0, n, Z_RV // 1024)),
                  pl.BlockSpec((B, C, 1024), lambda n: (0, n, Z_RG // 1024)),
                  pl.BlockSpec((C, LANES), lambda n: (n, 0)),
                  pl.BlockSpec((C, LANES), lambda n: (n, 0)),
                  pl.BlockSpec((RET_HEADS, LANES), lambda n: (0, 0))],
        out_specs=[pl.BlockSpec((B, C, 1024), lambda n: (0, n, 0)),
                   pl.BlockSpec((B, RET_HEADS, RET_DK, RET_DV), lambda n: (0, 0, 0, 0))],
        scratch_shapes=[pltpu.VMEM((B * RET_HEADS, RET_DK, RET_DV), F32)],
        compiler_params=_cp("arbitrary"),
        name="retention_prompt",
    )(z3, z3, z3, z3, cos_r, sin_r, lg)


def _ret_step_kernel(q_ref, k_ref, v_ref, rg_ref, s0_ref, cos_ref, sin_ref, lg_ref, o_ref, st_ref):
    cos, sin = cos_ref[...], sin_ref[...]
    eye = (lax.broadcasted_iota(I32, (RET_DK, RET_DK), 0) == lax.broadcasted_iota(I32, (RET_DK, RET_DK), 1))
    for h in range(RET_HEADS):
        qs = slice(h * RET_DK, (h + 1) * RET_DK)
        vs = slice(h * RET_DV, (h + 1) * RET_DV)
        q, k, v = q_ref[:, qs], k_ref[:, qs], v_ref[:, vs]
        qr = q * cos + pltpu.roll(q, RET_DK // 2, 1) * sin
        kr = (k * cos + pltpu.roll(k, RET_DK // 2, 1) * sin) * (RET_DK ** -0.5)
        lg = lg_ref[h:h + 1, :]
        gamma = jnp.exp(jnp.concatenate([lg, lg], axis=1))
        qcol = jnp.sum(jnp.where(eye, qr, 0.0), axis=1, keepdims=True)
        kcol = jnp.sum(jnp.where(eye, kr, 0.0), axis=1, keepdims=True)
        s0 = s0_ref[h]
        qk = jnp.sum(qr * kr, axis=1, keepdims=True)
        out = qk * v + jnp.sum(qcol * s0, axis=0, keepdims=True) * gamma
        st_ref[h] = gamma * s0 + kcol * v
        o_ref[:, vs] = (_silu(rg_ref[:, vs]) * _rms(out)).astype(BF16)


def _retention_sample(z3, state_ret, layer, cos_r, sin_r, lg):
    Bs = z3.shape[0]
    zc = lambda w, off: pl.BlockSpec((None, 1, w), lambda b, o=off // w: (b, 0, o))
    return pl.pallas_call(
        _ret_step_kernel,
        out_shape=[jax.ShapeDtypeStruct((Bs, 1, 1024), BF16),
                   jax.ShapeDtypeStruct((Bs, RET_HEADS, RET_DK, RET_DV), F32)],
        grid=(Bs,),
        in_specs=[zc(512, Z_RQ), zc(512, Z_RK), zc(1024, Z_RV), zc(1024, Z_RG),
                  pl.BlockSpec((None, None, RET_HEADS, RET_DK, RET_DV), lambda b: (layer, b, 0, 0, 0)),
                  pl.BlockSpec((1, LANES), lambda b: (0, 0)),
                  pl.BlockSpec((1, LANES), lambda b: (0, 0)),
                  pl.BlockSpec((RET_HEADS, LANES), lambda b: (0, 0))],
        out_specs=[pl.BlockSpec((None, 1, 1024), lambda b: (b, 0, 0)),
                   pl.BlockSpec((None, RET_HEADS, RET_DK, RET_DV), lambda b: (b, 0, 0, 0))],
        compiler_params=_cp("parallel"),
        name="retention_sample",
    )(z3, z3, z3, z3, state_ret, cos_r, sin_r, lg)


CONV_HALO = 32
CONV_RB = 32
CONV_CB = 512


def _conv_kernel(a_ref, g_ref, w_ref, b_ref, cg_ref, o_ref, cs_ref, u_ref, sh_ref, y_ref):
    t = pl.program_id(1)
    tm = a_ref.shape[0]

    @pl.when(t == 0)
    def _():
        u_ref[0:CONV_HALO, :] = jnp.zeros((CONV_HALO, u_ref.shape[1]), F32)

    u_ref[CONV_HALO:CONV_HALO + tm, :] = a_ref[...] * jax.nn.sigmoid(g_ref[...])
    span = tm + CONV_HALO - SUBLANES
    for s in range(1, SUBLANES):
        sh_ref[s - 1, 0:span, :] = u_ref[s:s + span, :]
    first = CONV_HALO - (CONV_W - 1)
    for r in range(tm // CONV_RB):
        for c in range(u_ref.shape[1] // CONV_CB):
            cs = slice(c * CONV_CB, (c + 1) * CONV_CB)
            acc = jnp.zeros((CONV_RB, CONV_CB), F32)
            for j in range(CONV_W):
                s = (first + j) % SUBLANES
                lo = r * CONV_RB + first + j - s
                src = u_ref if s == 0 else sh_ref.at[s - 1]
                acc = acc + w_ref[j:j + 1, cs] * src[lo:lo + CONV_RB, cs]
            y_ref[r * CONV_RB:(r + 1) * CONV_RB, cs] = acc
    cy = y_ref[...] + b_ref[...]
    o_ref[...] = _silu(_rms(cy) * cg_ref[...]).astype(BF16)

    @pl.when(t == pl.num_programs(1) - 1)
    def _():
        cs_ref[...] = u_ref[tm + first:tm + CONV_HALO, :]

    u_ref[0:CONV_HALO, :] = u_ref[tm:tm + CONV_HALO, :]


def _conv_prompt(z, B, T, conv_w, conv_b, conv_g, layer, tm):
    M, C = B * T, 1024
    nt = T // tm
    return pl.pallas_call(
        _conv_kernel,
        out_shape=[jax.ShapeDtypeStruct((M, C), BF16), jax.ShapeDtypeStruct((B, CONV_W - 1, C), F32)],
        grid=(B, nt),
        in_specs=[pl.BlockSpec((tm, C), lambda b, t: (b * nt + t, Z_CU // C)),
                  pl.BlockSpec((tm, C), lambda b, t: (b * nt + t, Z_CG // C)),
                  pl.BlockSpec((None, CONV_W, C), lambda b, t: (layer, 0, 0)),
                  pl.BlockSpec((None, 1, C), lambda b, t: (layer, 0, 0)),
                  pl.BlockSpec((None, 1, C), lambda b, t: (layer, 0, 0))],
        out_specs=[pl.BlockSpec((tm, C), lambda b, t: (b * nt + t, 0)),
                   pl.BlockSpec((None, CONV_W - 1, C), lambda b, t: (b, 0, 0))],
        scratch_shapes=[pltpu.VMEM((tm + CONV_HALO, C), F32), pltpu.VMEM((SUBLANES - 1, tm + CONV_HALO, C), F32),
                        pltpu.VMEM((tm, C), F32)],
        compiler_params=_cp("parallel", "arbitrary"),
        name="conv_prompt",
    )(z, z, conv_w, conv_b, conv_g)


def _conv_step_kernel(a_ref, g_ref, buf_ref, w_ref, b_ref, cg_ref, o_ref, cs_ref):
    u = a_ref[...] * jax.nn.sigmoid(g_ref[...])
    buf = buf_ref[...]
    y = jnp.sum(w_ref[0:CONV_W - 1, :] * buf, axis=0, keepdims=True) + w_ref[CONV_W - 1:CONV_W, :] * u
    cy = y + b_ref[...]
    o_ref[...] = _silu(_rms(cy) * cg_ref[...]).astype(BF16)
    cs_ref[0:CONV_W - 2, :] = buf[1:CONV_W - 1, :]
    cs_ref[CONV_W - 2:CONV_W - 1, :] = u


def _conv_sample(z3, state_conv, conv_w, conv_b, conv_g, layer):
    Bs, C = z3.shape[0], 1024
    return pl.pallas_call(
        _conv_step_kernel,
        out_shape=[jax.ShapeDtypeStruct((Bs, 1, C), BF16), jax.ShapeDtypeStruct((Bs, CONV_W - 1, C), F32)],
        grid=(Bs,),
        in_specs=[pl.BlockSpec((None, 1, C), lambda b: (b, 0, Z_CU // C)),
                  pl.BlockSpec((None, 1, C), lambda b: (b, 0, Z_CG // C)),
                  pl.BlockSpec((None, None, CONV_W - 1, C), lambda b: (layer, b, 0, 0)),
                  pl.BlockSpec((None, CONV_W, C), lambda b: (layer, 0, 0)),
                  pl.BlockSpec((None, 1, C), lambda b: (layer, 0, 0)),
                  pl.BlockSpec((None, 1, C), lambda b: (layer, 0, 0))],
        out_specs=[pl.BlockSpec((None, 1, C), lambda b: (b, 0, 0)),
                   pl.BlockSpec((None, CONV_W - 1, C), lambda b: (b, 0, 0))],
        compiler_params=_cp("parallel"),
        name="conv_sample",
    )(z3, z3, state_conv, conv_w, conv_b, conv_g)


def _sel_kernel(ik3_ref, iqh_ref, iql_ref, iw_ref, m_ref, key_ref, w_ref, j_ref, *, T, K, ck):
    i = pl.program_id(1)
    nq = LANES
    for h in range(IDX_HEADS):
        rs = slice(h * IDX_DIM, (h + 1) * IDX_DIM)
        cs = slice(h * nq, (h + 1) * nq)
        hi = iqh_ref[rs, :]
        w_ref[0:IDX_DIM, cs] = hi
        w_ref[IDX_DIM:2 * IDX_DIM, cs] = hi
        w_ref[2 * IDX_DIM:3 * IDX_DIM, cs] = iql_ref[rs, :]
        w_ref[3 * IDX_DIM:4 * IDX_DIM, cs] = jnp.zeros((IDX_DIM, nq), BF16)
    nch = ((i + 1) * nq + ck - 1) // ck
    tq = i * nq + lax.broadcasted_iota(I32, (ck, nq), 1)
    row = lax.broadcasted_iota(I32, (ck, nq), 0)
    iw = iw_ref[...]

    def score_body(c, carry):
        off = pl.multiple_of(c * ck, ck)
        d = _dot(ik3_ref[pl.ds(off, ck), :], w_ref[...])
        s = jnp.zeros((ck, nq), F32)
        for h in range(IDX_HEADS):
            s = s + jnp.maximum(d[:, h * nq:(h + 1) * nq], 0.0) * iw[h:h + 1, :]
        key_ref[pl.ds(off, ck), :] = jnp.where(off + row <= tq, _sortable(s), INT_MIN)
        return carry

    lax.fori_loop(0, nch, score_body, 0)

    def count(pred):
        def body(c, acc):
            off = pl.multiple_of(c * ck, ck)
            hit = jnp.where(pred(key_ref[pl.ds(off, ck), :], off), 1.0, 0.0)
            return acc + jnp.sum(hit.reshape(ck // COUNT_ROWS, COUNT_ROWS, nq), axis=0)
        acc = lax.fori_loop(0, nch, body, jnp.zeros((COUNT_ROWS, nq), F32))
        return jnp.sum(acc, axis=0, keepdims=True)

    def try_bit(tau, n_ge, cand):
        cnt = count(lambda kk, off: kk >= cand)
        ok = cnt >= K
        return jnp.where(ok, cand, tau), jnp.where(ok, cnt, n_ge)

    def unsettled(n_ge):
        return jnp.max(jnp.where(n_ge > K, 1, 0))

    def search_body(c):
        bit, _, tau, n_ge = c
        tau, n_ge = try_bit(tau, n_ge, tau | lax.shift_left(jnp.int32(1), bit))
        return bit - 1, unsettled(n_ge), tau, n_ge

    n_causal = (i * nq + lax.broadcasted_iota(I32, (1, nq), 1) + 1).astype(F32)
    tau, n_ge = try_bit(jnp.full((1, nq), INT_MIN, I32), n_causal, jnp.zeros((1, nq), I32))
    _, ties_left, tau, n_ge = lax.while_loop(lambda c: (c[0] >= 0) & (c[1] > 0), search_body,
                                             (jnp.int32(30), unsettled(n_ge), tau, n_ge))
    tau = jnp.maximum(tau, INT_MIN + 1)
    j_ref[...] = jnp.full((1, nq), 2 ** 30, I32)

    @pl.when(ties_left > 0)
    def _():
        need = K - count(lambda kk, off: kk > tau)
        jj = jnp.zeros((1, nq), I32)
        for bit in range(T.bit_length() - 1, -1, -1):
            cand = jj | (1 << bit)
            f = count(lambda kk, off, cand=cand: jnp.where(kk == tau, off + row, 2 ** 30) < cand)
            jj = jnp.where(f <= need, cand, jj)
        j_ref[...] = jj

    jlim = j_ref[...]

    def write_body(c, carry):
        off = pl.multiple_of(c * ck, ck)
        kk = key_ref[pl.ds(off, ck), :]
        tie_pos = jnp.where(kk == tau, off + row, 2 ** 30)
        bias = jnp.where(kk > tau, 0.0, jnp.where(tie_pos < jlim, 0.0, MASK_BIAS))
        m_ref[pl.ds(off, ck), :] = bias.astype(BF16)
        return carry

    lax.fori_loop(0, nch, write_body, 0)

    def fill_body(c, carry):
        off = pl.multiple_of(c * ck, ck)
        m_ref[pl.ds(off, ck), :] = jnp.full((ck, nq), MASK_BIAS, BF16)
        return carry

    lax.fori_loop(nch, T // ck, fill_body, 0)


def _select_prompt(ik3, iqth, iqtl, iwt, B, T, ck):
    nq = T // LANES
    K = min(TOPK_MAX, T // 4)
    return pl.pallas_call(
        functools.partial(_sel_kernel, T=T, K=K, ck=ck),
        out_shape=jax.ShapeDtypeStruct((B, T, T), BF16),
        grid=(B, nq),
        in_specs=[pl.BlockSpec((T, 256), lambda b, i: (b, 0)),
                  pl.BlockSpec((512, LANES), lambda b, i: (0, b * nq + i)),
                  pl.BlockSpec((512, LANES), lambda b, i: (0, b * nq + i)),
                  pl.BlockSpec((IDX_HEADS, LANES), lambda b, i: (0, b * nq + i))],
        out_specs=pl.BlockSpec((None, T, LANES), lambda b, i: (b, 0, i)),
        scratch_shapes=[pltpu.VMEM((T, LANES), I32), pltpu.VMEM((256, IDX_HEADS * LANES), BF16),
                        pltpu.VMEM((1, LANES), I32)],
        compiler_params=_cp("parallel", "arbitrary"),
        name="dsa_select_prompt",
    )(ik3, iqth, iqtl, iwt)


def _att_kernel(q_ref, k_ref, vt_ref, m_ref, o_ref, p_ref, s_ref, *, ck, big):
    i = pl.program_id(2)
    nq = LANES
    s_keys = (i + 1) * nq
    s_up = ((s_keys + ck - 1) // ck) * ck
    n_it = (s_keys + big - 1) // big
    q2 = q_ref[...]
    qs = jnp.concatenate([q2[:, :HEAD_DIM], q2[:, HEAD_DIM:]], axis=0)
    row = lax.broadcasted_iota(I32, (big, nq), 0)

    def scores(c):
        off = pl.multiple_of(jnp.maximum(jnp.minimum(c * big, s_up - big), 0), ck)
        bias = jnp.where(off + row >= c * big, m_ref[pl.ds(off, big), :].astype(F32), MASK_BIAS)
        return off, _dot_nt(k_ref[pl.ds(off, big), :], qs) + jnp.concatenate([bias, bias], axis=1)

    def softmax_step(sm, m, l):
        m_new = jnp.maximum(m, jnp.max(sm, axis=0, keepdims=True))
        alpha = jnp.exp(m - m_new)
        p = jnp.exp(sm - m_new)
        return m_new, alpha, l * alpha + jnp.sum(p, axis=0, keepdims=True), p.astype(BF16)

    def pv(off, slot):
        out = _dot(vt_ref[off // ck], p_ref[slot, 0:ck, :])
        for u in range(1, big // ck):
            out = out + _dot(vt_ref[off // ck + u], p_ref[slot, u * ck:(u + 1) * ck, :])
        return out

    off0, sm = scores(0)
    m, alpha, l, pb = softmax_step(sm, jnp.full((1, 2 * nq), M_INIT, F32), jnp.zeros((1, 2 * nq), F32))
    p_ref[0] = pb
    off1, sm = scores(jnp.minimum(1, n_it - 1))
    s_ref[1] = sm

    def body(c, carry):
        m, l, acc, alpha_prev, off_prev, off_cur = carry
        pending = pv(off_prev, (c - 1) % 2)
        m, alpha, l, pb = softmax_step(s_ref[c % 2], m, l)
        p_ref[c % 2] = pb
        off_next, sm_next = scores(jnp.minimum(c + 1, n_it - 1))
        s_ref[(c + 1) % 2] = sm_next
        return m, l, acc * alpha_prev + pending, alpha, off_cur, off_next

    _, l, acc, alpha_prev, off_prev, _ = lax.fori_loop(
        1, n_it, body, (m, l, jnp.zeros((HEAD_DIM, 2 * nq), F32), alpha, off0, off1))
    ot = (acc * alpha_prev + pv(off_prev, (n_it - 1) % 2)) / l
    o_ref[:, 0:HEAD_DIM] = ot[:, 0:nq].T.astype(BF16)
    o_ref[:, HEAD_DIM:2 * HEAD_DIM] = ot[:, nq:2 * nq].T.astype(BF16)


def _attend_prompt(qa, kb, vt3, mask, B, T, ck):
    nq = T // LANES
    M = B * T
    big = min(2 * ck, T)
    return pl.pallas_call(
        functools.partial(_att_kernel, ck=ck, big=big),
        out_shape=jax.ShapeDtypeStruct((M, ATT_HEADS * HEAD_DIM), BF16),
        grid=(B, KV_HEADS, nq),
        in_specs=[pl.BlockSpec((LANES, 2 * HEAD_DIM), lambda b, g, i: (b * nq + i, g)),
                  pl.BlockSpec((T, HEAD_DIM), lambda b, g, i: (b, g)),
                  pl.BlockSpec((T // ck, HEAD_DIM, ck), lambda b, g, i: (b, g, 0)),
                  pl.BlockSpec((None, T, LANES), lambda b, g, i: (b, 0, i))],
        out_specs=pl.BlockSpec((LANES, 2 * HEAD_DIM), lambda b, g, i: (b * nq + i, g)),
        scratch_shapes=[pltpu.VMEM((2, big, 2 * LANES), BF16), pltpu.VMEM((2, big, 2 * LANES), F32)],
        compiler_params=_cp("parallel", "parallel", "arbitrary"),
        name="dsa_attend_prompt",
    )(qa, kb, vt3, mask)


def _score_step_kernel(pt_ref, iqt_ref, iw_ref, *refs, pps):
    ik_refs = refs[:pps]
    sc_ref = refs[pps]
    p = pl.program_id(1)
    iqt = iqt_ref[...]
    iw = iw_ref[...]
    mult = [jnp.broadcast_to(iqt[:, h:h + 1], (IDX_DIM, PAGE)) for h in range(IDX_HEADS)]
    for j in range(pps):
        pg = ik_refs[j][...]
        s = jnp.zeros((1, PAGE), F32)
        for h in range(IDX_HEADS):
            d = jnp.sum(pg * mult[h], axis=0, keepdims=True)
            s = s + jnp.maximum(d, 0.0) * iw[h:h + 1, :]
        sc_ref[pl.ds(p * pps + j, 1), :] = s


def _threshold_step_kernel(sc_ref, iq_ref, iw_ref, iko_ref, msk_ref, own_ref, *, K):
    key = _sortable(sc_ref[...])
    n_pages = key.shape[1]
    iw = iw_ref[...]
    d_own = jnp.sum(iq_ref[...] * iko_ref[...], axis=2, keepdims=True)
    s_own = jnp.sum(jnp.maximum(d_own, 0.0) * iw, axis=1, keepdims=True)
    key_own = _sortable(s_own)
    pos = lax.broadcasted_iota(I32, key.shape, 1) * PAGE + lax.broadcasted_iota(I32, key.shape, 2)
    pos_own = n_pages * PAGE

    def count(pred):
        c = jnp.sum(jnp.where(pred(key, pos), 1.0, 0.0), axis=1, keepdims=True)
        c = jnp.sum(c, axis=2, keepdims=True)
        return c + jnp.where(pred(key_own, pos_own), 1.0, 0.0)

    tau = jnp.full(key_own.shape, INT_MIN, I32)
    for bit in range(31, -1, -1):
        cand = jnp.zeros(key_own.shape, I32) if bit == 31 else tau | (1 << bit)
        tau = jnp.where(count(lambda kk, ps, cand=cand: kk >= cand) >= K, cand, tau)
    tau = jnp.maximum(tau, INT_MIN + 1)
    need = K - count(lambda kk, ps: kk > tau)
    jj = jnp.zeros(key_own.shape, I32)
    for bit in range((pos_own + 1).bit_length() - 1, -1, -1):
        cand = jj | (1 << bit)
        f = count(lambda kk, ps, cand=cand: jnp.where(kk == tau, ps, 2 ** 30) < cand)
        jj = jnp.where(f <= need, cand, jj)

    def selected(kk, ps):
        return jnp.where(kk > tau, 1.0, jnp.where(jnp.where(kk == tau, ps, 2 ** 30) < jj, 1.0, 0.0))

    msk_ref[...] = selected(key, pos)
    own_ref[...] = jnp.broadcast_to(selected(key_own, pos_own), own_ref.shape)


def _select_sample(page_table, iq8, iw8, ik_own, cache_idx_t, layer, pps):
    Bs, n_pages = page_table.shape
    K = min(TOPK_MAX, (n_pages * PAGE + 1) // 4)
    page_spec = lambda j: pl.BlockSpec((None, None, IDX_DIM, PAGE),
                                       lambda b, p, pt, j=j: (layer, pt[b, p * pps + j], 0, 0))
    grid_spec = pltpu.PrefetchScalarGridSpec(
        num_scalar_prefetch=1,
        grid=(Bs, n_pages // pps),
        in_specs=[pl.BlockSpec((None, IDX_DIM, IDX_HEADS), lambda b, p, pt: (b, 0, 0)),
                  pl.BlockSpec((None, IDX_HEADS, 1), lambda b, p, pt: (b, 0, 0))]
                 + [page_spec(j) for j in range(pps)],
        out_specs=pl.BlockSpec((None, n_pages, PAGE), lambda b, p, pt: (b, 0, 0)),
    )
    scores = pl.pallas_call(
        functools.partial(_score_step_kernel, pps=pps),
        out_shape=jax.ShapeDtypeStruct((Bs, n_pages, PAGE), F32),
        grid_spec=grid_spec,
        compiler_params=_cp("parallel", "arbitrary"),
        name="dsa_score_sample",
    )(page_table, jnp.swapaxes(iq8, 1, 2), iw8, *([cache_idx_t] * pps))
    whole = lambda shape: pl.BlockSpec(shape, lambda i: (0,) * len(shape))
    return pl.pallas_call(
        functools.partial(_threshold_step_kernel, K=K),
        out_shape=[jax.ShapeDtypeStruct((Bs, n_pages, PAGE), F32), jax.ShapeDtypeStruct((Bs, 1, LANES), F32)],
        grid=(1,),
        in_specs=[whole((Bs, n_pages, PAGE)), whole((Bs, IDX_HEADS, IDX_DIM)), whole((Bs, IDX_HEADS, 1)),
                  whole((Bs, 1, IDX_DIM))],
        out_specs=[whole((Bs, n_pages, PAGE)), whole((Bs, 1, LANES))],
        compiler_params=_cp("arbitrary"),
        name="dsa_threshold_sample",
    )(scores, iq8, iw8, ik_own)


def _att_step_kernel(pt_ref, q_ref, msk_ref, own_ref, ko_ref, vo_ref, *refs, pp):
    k_refs, v_refs = refs[:pp], refs[pp:2 * pp]
    o_ref, m_sc, l_sc, acc_sc = refs[2 * pp:]
    p = pl.program_id(1)
    rows = PAGE * KV_HEADS
    grp = ATT_HEADS // KV_HEADS

    @pl.when(p == 0)
    def _():
        m_sc[...] = jnp.full(m_sc.shape, NEG, F32)
        l_sc[...] = jnp.zeros(l_sc.shape, F32)
        acc_sc[...] = jnp.zeros(acc_sc.shape, F32)

    q8 = q_ref[...]
    own_kv = (lax.broadcasted_iota(I32, (ATT_HEADS, rows), 1) % KV_HEADS
              == lax.broadcasted_iota(I32, (ATT_HEADS, rows), 0) // grp)
    expand = jnp.where(lax.broadcasted_iota(I32, (PAGE, rows), 1) // KV_HEADS
                       == lax.broadcasted_iota(I32, (PAGE, rows), 0), 1.0, 0.0).astype(BF16)
    sms = []
    for j in range(pp):
        s = _dot_nt(q8, k_refs[j][...].astype(BF16))
        mrow = jnp.broadcast_to(msk_ref[pl.ds(p * pp + j, 1), :], (ATT_HEADS, PAGE)).astype(BF16)
        picked = _dot(mrow, expand)
        sms.append(jnp.where(jnp.where(own_kv, picked, 0.0) > 0.0, s, NEG))
    m = m_sc[...]
    m_new = m
    for sm in sms:
        m_new = jnp.maximum(m_new, jnp.max(sm, axis=1, keepdims=True))
    alpha = jnp.exp(m - m_new)
    l = l_sc[...] * alpha
    acc = acc_sc[...] * alpha
    for j in range(pp):
        pr = jnp.where(sms[j] > 0.5 * NEG, jnp.exp(sms[j] - m_new), 0.0)
        l = l + jnp.sum(pr, axis=1, keepdims=True)
        acc = acc + _dot(pr.astype(BF16), v_refs[j][...].astype(BF16))
    m_sc[...], l_sc[...], acc_sc[...] = m_new, l, acc

    @pl.when(p == pl.num_programs(1) - 1)
    def _():
        ko = ko_ref[...].astype(BF16).astype(F32)
        s_own = jnp.sum(q8.astype(F32) * ko, axis=1, keepdims=True)
        sel_own = own_ref[:, 0:1] > 0.0
        sm = jnp.where(sel_own, s_own, NEG)
        m_fin = jnp.maximum(m_new, sm)
        a_fin = jnp.exp(m_new - m_fin)
        pr = jnp.where(sel_own, jnp.exp(sm - m_fin), 0.0)
        o = (acc * a_fin + pr * vo_ref[...]) / (l * a_fin + pr)
        o_ref[...] = o.astype(BF16)


def _attend_sample(page_table, q8, msk, own, k_own8, v_own8, cache_k, cache_v, layer, pp):
    Bs, n_pages = page_table.shape
    rows = PAGE * KV_HEADS
    page_spec = lambda j: pl.BlockSpec((None, None, rows, HEAD_DIM),
                                       lambda b, p, pt, j=j: (layer, pt[b, p * pp + j], 0, 0))
    per_b = lambda r, w: pl.BlockSpec((None, r, w), lambda b, p, pt: (b, 0, 0))
    grid_spec = pltpu.PrefetchScalarGridSpec(
        num_scalar_prefetch=1,
        grid=(Bs, n_pages // pp),
        in_specs=[per_b(ATT_HEADS, HEAD_DIM), per_b(n_pages, PAGE), per_b(1, LANES),
                  per_b(ATT_HEADS, HEAD_DIM), per_b(ATT_HEADS, HEAD_DIM)]
                 + [page_spec(j) for j in range(pp)] + [page_spec(j) for j in range(pp)],
        out_specs=per_b(ATT_HEADS, HEAD_DIM),
        scratch_shapes=[pltpu.VMEM((ATT_HEADS, 1), F32), pltpu.VMEM((ATT_HEADS, 1), F32),
                        pltpu.VMEM((ATT_HEADS, HEAD_DIM), F32)],
    )
    return pl.pallas_call(
        functools.partial(_att_step_kernel, pp=pp),
        out_shape=jax.ShapeDtypeStruct((Bs, ATT_HEADS, HEAD_DIM), BF16),
        grid_spec=grid_spec,
        compiler_params=_cp("parallel", "arbitrary"),
        name="dsa_attend_sample",
    )(page_table, q8, msk, own, k_own8, v_own8, *([cache_k] * pp), *([cache_v] * pp))


def _mixout_kernel(x_ref, ro_ref, ca_ref, ao_ref, g0_ref, g1_ref, g2_ref, gm_ref,
                   wr_ref, wc_ref, wa_ref, wo_ref, o_ref):
    merged = (jax.nn.sigmoid(g0_ref[...]) * _dot(ro_ref[...], wr_ref[...])
              + jax.nn.sigmoid(g1_ref[...]) * _dot(ca_ref[...], wc_ref[...])
              + jax.nn.sigmoid(g2_ref[...]) * _dot(ao_ref[...], wa_ref[...]))
    o_ref[...] = x_ref[...] + gm_ref[0] * _dot(merged.astype(BF16), wo_ref[...])


def _mixout(x, ro, ca, ao, z, gate, w_r, w_c, w_a, w_o, tm, rpg):
    M, D = x.shape
    row = pl.BlockSpec((tm, D), lambda i: (i, 0))
    glc = lambda k: pl.BlockSpec((tm, D), lambda i, o=Z_GL // D + k: (i, o))
    wsp = pl.BlockSpec((D, D), lambda i: (0, 0))
    return pl.pallas_call(
        _mixout_kernel,
        out_shape=jax.ShapeDtypeStruct((M, D), F32),
        grid=(M // tm,),
        in_specs=[row, row, row, row, glc(0), glc(1), glc(2), _mod_spec(gate, tm, rpg), wsp, wsp, wsp, wsp],
        out_specs=row,
        compiler_params=_cp("parallel"),
        name="mixer_out",
    )(x, ro, ca, ao, z, z, z, gate, w_r, w_c, w_a, w_o)


def _rope_tables(pos, rot_dim, theta, period):
    half = rot_dim // 2
    inv = 1.0 / (theta ** (jnp.arange(half, dtype=F32) / half))
    ang = pos.astype(F32)[:, None] * inv[None, :]
    lane = np.arange(LANES) % period
    idx = lane % half
    cos = jnp.where(lane < rot_dim, jnp.cos(ang)[:, idx], 1.0)
    sin = jnp.sin(ang)[:, idx]
    s_lo = jnp.where(lane < half, -sin, 0.0)
    s_hi = jnp.where((lane >= half) & (lane < rot_dim), sin, 0.0)
    return cos, s_lo, s_hi


def _tables(pos):
    ca = _rope_tables(pos, ROT_DIM, ROPE_THETA, HEAD_DIM)
    ci = _rope_tables(pos, IDX_ROT, ROPE_THETA, IDX_DIM)
    cr, r_lo, r_hi = _rope_tables(pos, RET_DK, RET_THETA, RET_DK)
    return ca, ci, (cr, r_lo + r_hi)


def _mods(mod, rows):
    parts = jnp.split(mod, 9, axis=-1)
    if rows == 1:
        return [p[:, None, :] for p in parts]
    return [p[None] for p in parts]


def _cat_w_in(w):
    pad = Z_END - Z_IX - (IDX_DIM + IDX_HEADS)
    return jnp.concatenate([w[:, :6144], w[:, 7752:10824], w[:, 6144:7680], w[:, 7680:7752],
                            jnp.zeros((w.shape[0], pad), w.dtype)], axis=1).astype(BF16)


def kernel(x_prompt, x_sample, cache_k, cache_v, cache_idx_k, state_ret, state_conv, page_table, c_prompt, c_sample, w_ada, b_ada, norm_g, w_ff1_up, w_ff1_down, w_ff2_up, w_ff2_down, w_in, conv_w, conv_b, conv_g, q_norm_g, k_norm_g, idx_k_norm_g, w_ret_o, w_conv_o, w_att_o, w_o):
    B, T, D = x_prompt.shape
    Bs, Ts, _ = x_sample.shape
    assert Ts == 1 and T % 512 == 0 and D == 1024
    depth = w_ada.shape[0]
    n_pages = page_table.shape[1]
    past = n_pages * PAGE
    tm_p = 512
    ck = 256
    pps = 16 if n_pages % 16 == 0 else 4
    pp = 8 if n_pages % 8 == 0 else 4
    grp = ATT_HEADS // KV_HEADS

    tabs_p = _tables(jnp.arange(T, dtype=jnp.int32))
    tabs_s = _tables(jnp.full((Bs,), past, dtype=jnp.int32))
    lg = jnp.broadcast_to(jnp.log(1.0 - 2.0 ** (-5.0 - jnp.arange(RET_HEADS, dtype=F32)))[:, None],
                          (RET_HEADS, LANES))
    c_all = jnp.concatenate([c_prompt, c_sample], axis=0)
    ck4 = cache_k.reshape(cache_k.shape[0], cache_k.shape[1], PAGE * KV_HEADS, HEAD_DIM)
    cv4 = cache_v.reshape(cache_v.shape[0], cache_v.shape[1], PAGE * KV_HEADS, HEAD_DIM)
    cache_idx_t = jnp.swapaxes(cache_idx_k, 2, 3)
    conv_b3 = conv_b.reshape(depth, 1, D)
    conv_g3 = conv_g.reshape(depth, 1, D)

    xp = x_prompt.reshape(B * T, D)
    xs = x_sample.reshape(Bs, D)
    outs_p, outs_s = [], []
    for l in range(depth):
        mod = _ada_mod(c_all, w_ada, b_ada, l)
        mp = _mods(mod[:B], 1)
        msm = _mods(mod[B:], Bs)
        wu1, wd1 = w_ff1_up[l].astype(BF16), w_ff1_down[l].astype(BF16)
        wu2, wd2 = w_ff2_up[l].astype(BF16), w_ff2_down[l].astype(BF16)
        w_cat = _cat_w_in(w_in[l])
        w_r, w_c = w_ret_o[l].astype(BF16), w_conv_o[l].astype(BF16)
        w_a, w_oo = w_att_o[l].astype(BF16), w_o[l].astype(BF16)
        ng = norm_g[l]
        qg, kg = q_norm_g[l][None, :], k_norm_g[l][None, :]
        ikg = jnp.concatenate([idx_k_norm_g[l], jnp.zeros((LANES - IDX_DIM,), F32)])[None, :]

        xp = _ffn(xp, mp[0], mp[1], mp[2], ng[0:1], wu1, wd1, tm_p, T)
        z = _inproj(xp, mp[3], mp[4], ng[1:2], w_cat, min(1024, T), T)
        (qa, k_p, kb, v_p, _iqr, _ixo, idxk_p, vt3, iqth, iqtl, ik3, iwt) = _prep(
            z, tabs_p[0], tabs_p[1], qg, kg, ikg, tm_p, True, ck)
        ro, ret_p = _retention_prompt(z.reshape(B, T, Z_END), tabs_p[2][0], tabs_p[2][1], lg)
        ca, conv_p = _conv_prompt(z, B, T, conv_w, conv_b3, conv_g3, l, tm_p)
        mask = _select_prompt(ik3, iqth, iqtl, iwt, B, T, 2 * ck)
        ao = _attend_prompt(qa, kb, vt3, mask, B, T, ck)
        xp = _mixout(xp, ro.reshape(B * T, D), ca, ao, z, mp[5], w_r, w_c, w_a, w_oo, tm_p, T)
        xp = _ffn(xp, mp[6], mp[7], mp[8], ng[2:3], wu2, wd2, tm_p, T)
        outs_p.append((k_p.reshape(B, T, KV_HEADS, HEAD_DIM), v_p.reshape(B, T, KV_HEADS, HEAD_DIM),
                       idxk_p.reshape(B, T, IDX_DIM), ret_p, conv_p))

        xs = _ffn(xs, msm[0], msm[1], msm[2], ng[0:1], wu1, wd1, Bs, Bs)
        zs = _inproj(xs, msm[3], msm[4], ng[1:2], w_cat, Bs, Bs)
        qa_s, k_s, _kb, v_s, iqr_s, ixo_s, idxk_s = _prep(
            zs, tabs_s[0], tabs_s[1], qg, kg, ikg, Bs, False, ck)
        zs3 = zs.reshape(Bs, 1, Z_END)
        ro_s, ret_s = _retention_sample(zs3, state_ret, l, tabs_s[2][0][0:1], tabs_s[2][1][0:1], lg)
        ca_s, conv_s = _conv_sample(zs3, state_conv, conv_w, conv_b3, conv_g3, l)
        iq8 = iqr_s.reshape(Bs, IDX_HEADS, IDX_DIM)
        iw8 = ixo_s[:, IDX_DIM:IDX_DIM + IDX_HEADS].reshape(Bs, IDX_HEADS, 1)
        msk, own = _select_sample(page_table, iq8, iw8, idxk_s.reshape(Bs, 1, IDX_DIM), cache_idx_t, l, pps)
        k_own8 = jnp.repeat(k_s.reshape(Bs, KV_HEADS, HEAD_DIM), grp, axis=1)
        v_own8 = jnp.repeat(v_s.reshape(Bs, KV_HEADS, HEAD_DIM), grp, axis=1)
        ao_s = _attend_sample(page_table, qa_s.reshape(Bs, ATT_HEADS, HEAD_DIM), msk, own,
                              k_own8, v_own8, ck4, cv4, l, pp)
        xs = _mixout(xs, ro_s.reshape(Bs, D), ca_s.reshape(Bs, D), ao_s.reshape(Bs, D), zs, msm[5],
                     w_r, w_c, w_a, w_oo, Bs, Bs)
        xs = _ffn(xs, msm[6], msm[7], msm[8], ng[2:3], wu2, wd2, Bs, Bs)
        outs_s.append((k_s.reshape(Bs, 1, KV_HEADS, HEAD_DIM), v_s.reshape(Bs, 1, KV_HEADS, HEAD_DIM),
                       idxk_s.reshape(Bs, 1, IDX_DIM), ret_s, conv_s))

    stack = lambda outs, j: jnp.stack([o[j] for o in outs])
    return (xp.reshape(B, T, D), xs.reshape(Bs, 1, D),
            stack(outs_p, 0), stack(outs_p, 1), stack(outs_p, 2), stack(outs_p, 3), stack(outs_p, 4),
            stack(outs_s, 0), stack(outs_s, 1), stack(outs_s, 2), stack(outs_s, 3), stack(outs_s, 4))
```

```python
import functools

import numpy as np
import jax
import jax.numpy as jnp
from jax import lax
from jax.experimental import pallas as pl
from jax.experimental.pallas import tpu as pltpu

F32 = jnp.float32
BF16 = jnp.bfloat16
I32 = jnp.int32

EPS = 1e-6
LANES = 128
SUBLANES = 8
NEG = -1e30
MASK_BIAS = -1e30
M_INIT = -1e20
INT_MIN = -2 ** 31
COUNT_ROWS = 64
RET_HEADS = 4
RET_DK = 128
RET_DV = 256
RET_CHUNK = 128
RET_THETA = 10000.0
CONV_W = 31
ATT_HEADS = 8
KV_HEADS = 4
HEAD_DIM = 128
IDX_HEADS = 8
IDX_DIM = 64
TOPK_MAX = 256
ROPE_THETA = 500000.0
ROT_DIM = HEAD_DIM // 4
IDX_ROT = IDX_DIM // 4
PAGE = 128

Z_RQ, Z_RK, Z_RV, Z_RG = 0, 512, 1024, 2048
Z_CU, Z_CG, Z_AQ, Z_GL = 3072, 4096, 5120, 6144
Z_AK, Z_AV, Z_IQ, Z_IX, Z_END = 9216, 9728, 10240, 10752, 10880

VMEM_LIMIT = 56 * 1024 * 1024


def _cp(*sem):
    return pltpu.CompilerParams(dimension_semantics=tuple(sem), vmem_limit_bytes=VMEM_LIMIT)


def _dot(a, b):
    return jnp.dot(a, b, preferred_element_type=F32)


def _dot_nt(a, b):
    return lax.dot_general(a, b, (((1,), (1,)), ((), ())), preferred_element_type=F32)


def _split(x):
    hi = x.astype(BF16)
    lo = (x - hi.astype(F32)).astype(BF16)
    return hi, lo


def _dot3(a, b, nt=False):
    d = _dot_nt if nt else _dot
    ah, al = _split(a)
    bh, bl = _split(b)
    return d(ah, bh) + d(ah, bl) + d(al, bh)


def _silu(x):
    return x * jax.nn.sigmoid(x)


def _rms(x):
    return x * lax.rsqrt(jnp.mean(x * x, axis=-1, keepdims=True) + EPS)


def _norm_mod(x, g, shift, scale):
    return _rms(x) * g * (1.0 + scale) + shift


def _rot(x, cos, s_lo, s_hi, half):
    return x * cos + pltpu.roll(x, LANES - half, 1) * s_lo + pltpu.roll(x, half, 1) * s_hi


def _sortable(s):
    s = jnp.where(s == 0.0, 0.0, s)
    bits = pltpu.bitcast(s, I32)
    return bits ^ ((bits >> 31) & 0x7FFFFFFF)


def _ada_kernel(c_ref, w_ref, b_ref, o_ref):
    o_ref[...] = _dot3(_silu(c_ref[...]), w_ref[...]) + b_ref[...]


def _ada_mod(c, w_ada, b_ada, layer):
    R, D = c.shape
    N = w_ada.shape[2]
    tn = 1024
    return pl.pallas_call(
        _ada_kernel,
        out_shape=jax.ShapeDtypeStruct((R, N), F32),
        grid=(N // tn,),
        in_specs=[pl.BlockSpec((R, D), lambda j: (0, 0)),
                  pl.BlockSpec((None, D, tn), lambda j: (layer, 0, j)),
                  pl.BlockSpec((None, 1, tn), lambda j: (layer, 0, j))],
        out_specs=pl.BlockSpec((R, tn), lambda j: (0, j)),
        compiler_params=_cp("parallel"),
        name="ada_mod",
    )(c, w_ada, b_ada.reshape(b_ada.shape[0], 1, N))


def _ffn_kernel(x_ref, sh_ref, sc_ref, g_ref, ng_ref, wu_ref, wd_ref, o_ref, acc_ref, *, ff, tf):
    x = x_ref[...]
    h = _norm_mod(x, ng_ref[...], sh_ref[0], sc_ref[0]).astype(BF16)
    for j in range(ff // tf):
        a = _dot(h, wu_ref[:, j * tf:(j + 1) * tf])
        b = _dot(h, wu_ref[:, ff + j * tf:ff + (j + 1) * tf])
        act = (_silu(a) * b).astype(BF16)
        upd = _dot(act, wd_ref[j * tf:(j + 1) * tf, :])
        if j == 0:
            acc_ref[...] = upd
        else:
            acc_ref[...] += upd
    o_ref[...] = x + 0.5 * g_ref[0] * acc_ref[...]


def _mod_spec(mod, tm, rpg):
    rm = mod.shape[1]
    return pl.BlockSpec((1, rm, mod.shape[2]), lambda i: ((i * tm) // rpg, 0, 0))


def _ffn(x, shift, scale, gate, ng, w_up, w_down, tm, rpg):
    M, D = x.shape
    ff = w_down.shape[0]
    const = lambda i: (0, 0)
    return pl.pallas_call(
        functools.partial(_ffn_kernel, ff=ff, tf=256),
        out_shape=jax.ShapeDtypeStruct((M, D), F32),
        grid=(M // tm,),
        in_specs=[pl.BlockSpec((tm, D), lambda i: (i, 0)),
                  _mod_spec(shift, tm, rpg), _mod_spec(scale, tm, rpg), _mod_spec(gate, tm, rpg),
                  pl.BlockSpec((1, D), const),
                  pl.BlockSpec((D, 2 * ff), const),
                  pl.BlockSpec((ff, D), const)],
        out_specs=pl.BlockSpec((tm, D), lambda i: (i, 0)),
        scratch_shapes=[pltpu.VMEM((tm, D), F32)],
        compiler_params=_cp("parallel"),
        name="ffn",
    )(x, shift, scale, gate, ng, w_up, w_down)


def _inproj_kernel(x_ref, sh_ref, sc_ref, ng_ref, w_ref, o_ref, h_ref):
    @pl.when(pl.program_id(1) == 0)
    def _():
        h_ref[...] = _norm_mod(x_ref[...], ng_ref[...], sh_ref[0], sc_ref[0]).astype(BF16)

    o_ref[...] = _dot(h_ref[...], w_ref[...])


def _inproj(x, shift, scale, ng, w_cat, tm, rpg):
    M, D = x.shape
    N = w_cat.shape[1]
    tn = N // 5
    ms = lambda mod: pl.BlockSpec((1, mod.shape[1], D), lambda i, j: ((i * tm) // rpg, 0, 0))
    return pl.pallas_call(
        _inproj_kernel,
        out_shape=jax.ShapeDtypeStruct((M, N), F32),
        grid=(M // tm, N // tn),
        in_specs=[pl.BlockSpec((tm, D), lambda i, j: (i, 0)),
                  ms(shift), ms(scale),
                  pl.BlockSpec((1, D), lambda i, j: (0, 0)),
                  pl.BlockSpec((D, tn), lambda i, j: (0, j))],
        out_specs=pl.BlockSpec((tm, tn), lambda i, j: (i, j)),
        scratch_shapes=[pltpu.VMEM((tm, D), BF16)],
        compiler_params=_cp("parallel", "arbitrary"),
        name="inproj",
    )(x, shift, scale, ng, w_cat)


def _prep_kernel(aq_ref, ak_ref, av_ref, iq_ref, ix_ref,
                 ca_ref, sa1_ref, sa2_ref, ci_ref, si1_ref, si2_ref, qg_ref, kg_ref, ikg_ref,
                 qa_ref, k_ref, kb_ref, v_ref, iqr_ref, ixo_ref, idxk_ref, *t_refs, tck):
    ca, sa1, sa2 = ca_ref[...], sa1_ref[...], sa2_ref[...]
    ci, si1, si2 = ci_ref[...], si1_ref[...], si2_ref[...]
    qg, kg = qg_ref[...], kg_ref[...]
    q_scale = HEAD_DIM ** -0.5
    for h in range(ATT_HEADS):
        sl = slice(h * HEAD_DIM, (h + 1) * HEAD_DIM)
        y = _rot(_rms(aq_ref[:, sl]) * qg, ca, sa1, sa2, ROT_DIM // 2)
        qa_ref[:, sl] = (y * q_scale).astype(BF16)
    for h in range(KV_HEADS):
        sl = slice(h * HEAD_DIM, (h + 1) * HEAD_DIM)
        y = _rot(_rms(ak_ref[:, sl]) * kg, ca, sa1, sa2, ROT_DIM // 2)
        k_ref[:, sl] = y
        kb_ref[:, sl] = y.astype(BF16)
    av = av_ref[...]
    v_ref[...] = av
    for j in range(IDX_HEADS * IDX_DIM // LANES):
        sl = slice(j * LANES, (j + 1) * LANES)
        iqr_ref[:, sl] = _rot(iq_ref[:, sl], ci, si1, si2, IDX_ROT // 2)
    blk = ix_ref[...]
    lane = lax.broadcasted_iota(I32, blk.shape, 1)
    ikm = jnp.where(lane < IDX_DIM, blk, 0.0)
    ms = jnp.sum(ikm * ikm, axis=-1, keepdims=True) * (1.0 / IDX_DIM)
    ikr = _rot(ikm * lax.rsqrt(ms + EPS) * ikg_ref[...], ci, si1, si2, IDX_ROT // 2)
    iw_scale = IDX_HEADS ** -0.5 * IDX_DIM ** -0.5
    iws = jnp.where((lane >= IDX_DIM) & (lane < IDX_DIM + IDX_HEADS), blk * iw_scale, 0.0)
    ixo = ikr + iws
    ixo_ref[...] = ixo
    idxk_ref[...] = ikr[:, :IDX_DIM]
    if t_refs:
        vt_ref, iqth_ref, iqtl_ref, ik3_ref, iwt_ref = t_refs
        tm = av.shape[0]
        avt = av.T
        for c in range(tm // tck):
            vt_ref[c] = avt[:, c * tck:(c + 1) * tck].astype(BF16)
        iqt = iqr_ref[...].T
        hi = iqt.astype(BF16)
        iqth_ref[...] = hi
        iqtl_ref[...] = (iqt - hi.astype(F32)).astype(BF16)
        khi = ikr.astype(BF16).astype(F32)
        klo = ikr - khi
        ik3_ref[...] = jnp.concatenate([khi + pltpu.roll(klo, IDX_DIM, 1), khi], axis=1).astype(BF16)
        iwt_ref[...] = ixo.T[IDX_DIM:IDX_DIM + IDX_HEADS, :]


def _prep(z, tabs_a, tabs_i, qg, kg, ikg, tm, transposed, tck):
    M = z.shape[0]
    nt = M // tm
    tpb = tabs_a[0].shape[0] // tm
    tab = pl.BlockSpec((tm, LANES), lambda i: (i % tpb, 0))
    col = lambda w, off: pl.BlockSpec((tm, w), lambda i, o=off // w: (i, o))
    row = lambda w: pl.BlockSpec((tm, w), lambda i: (i, 0))
    vec = pl.BlockSpec((1, LANES), lambda i: (0, 0))
    in_specs = [col(1024, Z_AQ), col(512, Z_AK), col(512, Z_AV), col(512, Z_IQ), col(128, Z_IX)]
    in_specs += [tab] * 6 + [vec] * 3
    out_shape = [jax.ShapeDtypeStruct((M, 1024), BF16), jax.ShapeDtypeStruct((M, 512), F32),
                 jax.ShapeDtypeStruct((M, 512), BF16), jax.ShapeDtypeStruct((M, 512), F32),
                 jax.ShapeDtypeStruct((M, 512), F32), jax.ShapeDtypeStruct((M, LANES), F32),
                 jax.ShapeDtypeStruct((M, IDX_DIM), F32)]
    out_specs = [row(1024), row(512), row(512), row(512), row(512), row(LANES), row(IDX_DIM)]
    if transposed:
        out_shape += [jax.ShapeDtypeStruct((M // tck, 512, tck), BF16),
                      jax.ShapeDtypeStruct((512, M), BF16), jax.ShapeDtypeStruct((512, M), BF16),
                      jax.ShapeDtypeStruct((M, 256), BF16), jax.ShapeDtypeStruct((IDX_HEADS, M), F32)]
        out_specs += [pl.BlockSpec((tm // tck, 512, tck), lambda i: (i, 0, 0)),
                      pl.BlockSpec((512, tm), lambda i: (0, i)), pl.BlockSpec((512, tm), lambda i: (0, i)),
                      row(256), pl.BlockSpec((IDX_HEADS, tm), lambda i: (0, i))]
    return pl.pallas_call(
        functools.partial(_prep_kernel, tck=tck),
        out_shape=out_shape, grid=(nt,), in_specs=in_specs, out_specs=out_specs,
        compiler_params=_cp("parallel"), name="dsa_prep",
    )(z, z, z, z, z, *tabs_a, *tabs_i, qg, kg, ikg)


def _ret_kernel(q_ref, k_ref, v_ref, rg_ref, cos_ref, sin_ref, lg_ref, o_ref, st_ref, s_ref, *, nb):
    n = pl.program_id(0)
    C = RET_CHUNK

    @pl.when(n == 0)
    def _():
        s_ref[...] = jnp.zeros_like(s_ref)

    cos, sin = cos_ref[...], sin_ref[...]
    ri = lax.broadcasted_iota(I32, (C, C), 0).astype(F32)
    ci = lax.broadcasted_iota(I32, (C, C), 1).astype(F32)
    diff = ri - ci
    rv = lax.broadcasted_iota(I32, (C, RET_DV), 0).astype(F32)
    for h in range(RET_HEADS):
        lg = lg_ref[h:h + 1, :]
        lg2 = jnp.concatenate([lg, lg], axis=1)
        dmask = jnp.where(diff >= 0, jnp.exp(jnp.maximum(diff, 0.0) * lg), 0.0)
        cross = jnp.exp((rv + 1.0) * lg2)
        kdec = jnp.exp((C - 1.0 - ri) * lg)
        g_c = jnp.exp(C * lg2)
        for b in range(nb):
            qs = slice(h * RET_DK, (h + 1) * RET_DK)
            vs = slice(h * RET_DV, (h + 1) * RET_DV)
            q = q_ref[b, :, qs]
            k = k_ref[b, :, qs]
            v = v_ref[b, :, vs]
            qr = q * cos + pltpu.roll(q, RET_DK // 2, 1) * sin
            kr = (k * cos + pltpu.roll(k, RET_DK // 2, 1) * sin) * (RET_DK ** -0.5)
            s_old = s_ref[b * RET_HEADS + h]
            sc = _dot3(qr, kr, nt=True) * dmask
            out = _dot3(sc, v) + _dot3(qr, s_old) * cross
            s_ref[b * RET_HEADS + h] = g_c * s_old + _dot3((kr * kdec).T, v)
            rg = rg_ref[b, :, vs]
            o_ref[b, :, vs] = (_silu(rg) * _rms(out)).astype(BF16)

    @pl.when(n == pl.num_programs(0) - 1)
    def _():
        for b in range(nb):
            for h in range(RET_HEADS):
                st_ref[b, h] = s_ref[b * RET_HEADS + h]


def _retention_prompt(z3, cos_r, sin_r, lg):
    B, T, _ = z3.shape
    C = RET_CHUNK
    return pl.pallas_call(
        functools.partial(_ret_kernel, nb=B),
        out_shape=[jax.ShapeDtypeStruct((B, T, 1024), BF16),
                   jax.ShapeDtypeStruct((B, RET_HEADS, RET_DK, RET_DV), F32)],
        grid=(T // C,),
        in_specs=[pl.BlockSpec((B, C, 512), lambda n: (0, n, Z_RQ // 512)),
                  pl.BlockSpec((B, C, 512), lambda n: (0, n, Z_RK // 512)),
                  pl.BlockSpec((B, C, 1024), lambda n: (0, n, Z_RV // 1024)),
                  pl.BlockSpec((B, C, 1024), lambda n: (0, n, Z_RG // 1024)),
                  pl.BlockSpec((C, LANES), lambda n: (n, 0)),
                  pl.BlockSpec((C, LANES), lambda n: (n, 0)),
                  pl.BlockSpec((RET_HEADS, LANES), lambda n: (0, 0))],
        out_specs=[pl.BlockSpec((B, C, 1024), lambda n: (0, n, 0)),
                   pl.BlockSpec((B, RET_HEADS, RET_DK, RET_DV), lambda n: (0, 0, 0, 0))],
        scratch_shapes=[pltpu.VMEM((B * RET_HEADS, RET_DK, RET_DV), F32)],
        compiler_params=_cp("arbitrary"),
        name="retention_prompt",
    )(z3, z3, z3, z3, cos_r, sin_r, lg)


def _ret_step_kernel(q_ref, k_ref, v_ref, rg_ref, s0_ref, cos_ref, sin_ref, lg_ref, o_ref, st_ref):
    cos, sin = cos_ref[...], sin_ref[...]
    eye = (lax.broadcasted_iota(I32, (RET_DK, RET_DK), 0) == lax.broadcasted_iota(I32, (RET_DK, RET_DK), 1))
    for h in range(RET_HEADS):
        qs = slice(h * RET_DK, (h + 1) * RET_DK)
        vs = slice(h * RET_DV, (h + 1) * RET_DV)
        q, k, v = q_ref[:, qs], k_ref[:, qs], v_ref[:, vs]
        qr = q * cos + pltpu.roll(q, RET_DK // 2, 1) * sin
        kr = (k * cos + pltpu.roll(k, RET_DK // 2, 1) * sin) * (RET_DK ** -0.5)
        lg = lg_ref[h:h + 1, :]
        gamma = jnp.exp(jnp.concatenate([lg, lg], axis=1))
        qcol = jnp.sum(jnp.where(eye, qr, 0.0), axis=1, keepdims=True)
        kcol = jnp.sum(jnp.where(eye, kr, 0.0), axis=1, keepdims=True)
        s0 = s0_ref[h]
        qk = jnp.sum(qr * kr, axis=1, keepdims=True)
        out = qk * v + jnp.sum(qcol * s0, axis=0, keepdims=True) * gamma
        st_ref[h] = gamma * s0 + kcol * v
        o_ref[:, vs] = (_silu(rg_ref[:, vs]) * _rms(out)).astype(BF16)


def _retention_sample(z3, state_ret, layer, cos_r, sin_r, lg):
    Bs = z3.shape[0]
    zc = lambda w, off: pl.BlockSpec((None, 1, w), lambda b, o=off // w: (b, 0, o))
    return pl.pallas_call(
        _ret_step_kernel,
        out_shape=[jax.ShapeDtypeStruct((Bs, 1, 1024), BF16),
                   jax.ShapeDtypeStruct((Bs, RET_HEADS, RET_DK, RET_DV), F32)],
        grid=(Bs,),
        in_specs=[zc(512, Z_RQ), zc(512, Z_RK), zc(1024, Z_RV), zc(1024, Z_RG),
                  pl.BlockSpec((None, None, RET_HEADS, RET_DK, RET_DV), lambda b: (layer, b, 0, 0, 0)),
                  pl.BlockSpec((1, LANES), lambda b: (0, 0)),
                  pl.BlockSpec((1, LANES), lambda b: (0, 0)),
                  pl.BlockSpec((RET_HEADS, LANES), lambda b: (0, 0))],
        out_specs=[pl.BlockSpec((None, 1, 1024), lambda b: (b, 0, 0)),
                   pl.BlockSpec((None, RET_HEADS, RET_DK, RET_DV), lambda b: (b, 0, 0, 0))],
        compiler_params=_cp("parallel"),
        name="retention_sample",
    )(z3, z3, z3, z3, state_ret, cos_r, sin_r, lg)


CONV_HALO = 32
CONV_RB = 32
CONV_CB = 512


def _conv_kernel(a_ref, g_ref, w_ref, b_ref, cg_ref, o_ref, cs_ref, u_ref, sh_ref, y_ref, wb_ref):
    t = pl.program_id(1)
    tm = a_ref.shape[0]

    @pl.when(t == 0)
    def _():
        u_ref[0:CONV_HALO, :] = jnp.zeros((CONV_HALO, u_ref.shape[1]), F32)
        for j in range(CONV_W):
            wb_ref[j] = jnp.broadcast_to(w_ref[j:j + 1, :], (SUBLANES, w_ref.shape[1]))

    u_ref[CONV_HALO:CONV_HALO + tm, :] = a_ref[...] * jax.nn.sigmoid(g_ref[...])
    span = tm + CONV_HALO - SUBLANES
    for s in range(1, SUBLANES):
        sh_ref[s - 1, 0:span, :] = u_ref[s:s + span, :]
    first = CONV_HALO - (CONV_W - 1)
    for r in range(tm // CONV_RB):
        for c in range(u_ref.shape[1] // CONV_CB):
            cs = slice(c * CONV_CB, (c + 1) * CONV_CB)
            acc = jnp.zeros((CONV_RB // SUBLANES, SUBLANES, CONV_CB), F32)
            for j in range(CONV_W):
                s = (first + j) % SUBLANES
                lo = r * CONV_RB + first + j - s
                src = u_ref if s == 0 else sh_ref.at[s - 1]
                tap = src[lo:lo + CONV_RB, cs].reshape(CONV_RB // SUBLANES, SUBLANES, CONV_CB)
                acc = acc + wb_ref[j, :, cs][None] * tap
            y_ref[r * CONV_RB:(r + 1) * CONV_RB, cs] = acc.reshape(CONV_RB, CONV_CB)
    cy = y_ref[...] + b_ref[...]
    o_ref[...] = _silu(_rms(cy) * cg_ref[...]).astype(BF16)

    @pl.when(t == pl.num_programs(1) - 1)
    def _():
        cs_ref[...] = u_ref[tm + first:tm + CONV_HALO, :]

    u_ref[0:CONV_HALO, :] = u_ref[tm:tm + CONV_HALO, :]


def _conv_prompt(z, B, T, conv_w, conv_b, conv_g, layer, tm):
    M, C = B * T, 1024
    nt = T // tm
    return pl.pallas_call(
        _conv_kernel,
        out_shape=[jax.ShapeDtypeStruct((M, C), BF16), jax.ShapeDtypeStruct((B, CONV_W - 1, C), F32)],
        grid=(B, nt),
        in_specs=[pl.BlockSpec((tm, C), lambda b, t: (b * nt + t, Z_CU // C)),
                  pl.BlockSpec((tm, C), lambda b, t: (b * nt + t, Z_CG // C)),
                  pl.BlockSpec((None, CONV_W, C), lambda b, t: (layer, 0, 0)),
                  pl.BlockSpec((None, 1, C), lambda b, t: (layer, 0, 0)),
                  pl.BlockSpec((None, 1, C), lambda b, t: (layer, 0, 0))],
        out_specs=[pl.BlockSpec((tm, C), lambda b, t: (b * nt + t, 0)),
                   pl.BlockSpec((None, CONV_W - 1, C), lambda b, t: (b, 0, 0))],
        scratch_shapes=[pltpu.VMEM((tm + CONV_HALO, C), F32), pltpu.VMEM((SUBLANES - 1, tm + CONV_HALO, C), F32),
                        pltpu.VMEM((tm, C), F32), pltpu.VMEM((CONV_W, SUBLANES, C), F32)],
        compiler_params=_cp("parallel", "arbitrary"),
        name="conv_prompt",
    )(z, z, conv_w, conv_b, conv_g)


def _conv_step_kernel(a_ref, g_ref, buf_ref, w_ref, b_ref, cg_ref, o_ref, cs_ref):
    u = a_ref[...] * jax.nn.sigmoid(g_ref[...])
    buf = buf_ref[...]
    y = jnp.sum(w_ref[0:CONV_W - 1, :] * buf, axis=0, keepdims=True) + w_ref[CONV_W - 1:CONV_W, :] * u
    cy = y + b_ref[...]
    o_ref[...] = _silu(_rms(cy) * cg_ref[...]).astype(BF16)
    cs_ref[0:CONV_W - 2, :] = buf[1:CONV_W - 1, :]
    cs_ref[CONV_W - 2:CONV_W - 1, :] = u


def _conv_sample(z3, state_conv, conv_w, conv_b, conv_g, layer):
    Bs, C = z3.shape[0], 1024
    return pl.pallas_call(
        _conv_step_kernel,
        out_shape=[jax.ShapeDtypeStruct((Bs, 1, C), BF16), jax.ShapeDtypeStruct((Bs, CONV_W - 1, C), F32)],
        grid=(Bs,),
        in_specs=[pl.BlockSpec((None, 1, C), lambda b: (b, 0, Z_CU // C)),
                  pl.BlockSpec((None, 1, C), lambda b: (b, 0, Z_CG // C)),
                  pl.BlockSpec((None, None, CONV_W - 1, C), lambda b: (layer, b, 0, 0)),
                  pl.BlockSpec((None, CONV_W, C), lambda b: (layer, 0, 0)),
                  pl.BlockSpec((None, 1, C), lambda b: (layer, 0, 0)),
                  pl.BlockSpec((None, 1, C), lambda b: (layer, 0, 0))],
        out_specs=[pl.BlockSpec((None, 1, C), lambda b: (b, 0, 0)),
                   pl.BlockSpec((None, CONV_W - 1, C), lambda b: (b, 0, 0))],
        compiler_params=_cp("parallel"),
        name="conv_sample",
    )(z3, z3, state_conv, conv_w, conv_b, conv_g)


def _sel_kernel(ik3_ref, iqh_ref, iql_ref, iw_ref, m_ref, key_ref, w_ref, j_ref, *, T, K, ck):
    i = pl.program_id(1)
    nq = LANES
    for h in range(IDX_HEADS):
        rs = slice(h * IDX_DIM, (h + 1) * IDX_DIM)
        cs = slice(h * nq, (h + 1) * nq)
        hi = iqh_ref[rs, :]
        w_ref[0:IDX_DIM, cs] = hi
        w_ref[IDX_DIM:2 * IDX_DIM, cs] = hi
        w_ref[2 * IDX_DIM:3 * IDX_DIM, cs] = iql_ref[rs, :]
        w_ref[3 * IDX_DIM:4 * IDX_DIM, cs] = jnp.zeros((IDX_DIM, nq), BF16)
    nch = ((i + 1) * nq + ck - 1) // ck
    tq = i * nq + lax.broadcasted_iota(I32, (ck, nq), 1)
    row = lax.broadcasted_iota(I32, (ck, nq), 0)
    iw = iw_ref[...]

    def score_body(c, carry):
        off = pl.multiple_of(c * ck, ck)
        d = _dot(ik3_ref[pl.ds(off, ck), :], w_ref[...])
        s = jnp.zeros((ck, nq), F32)
        for h in range(IDX_HEADS):
            s = s + jnp.maximum(d[:, h * nq:(h + 1) * nq], 0.0) * iw[h:h + 1, :]
        key_ref[pl.ds(off, ck), :] = jnp.where(off + row <= tq, _sortable(s), INT_MIN)
        return carry

    lax.fori_loop(0, nch, score_body, 0)

    def count(pred):
        def body(c, acc):
            off = pl.multiple_of(c * ck, ck)
            hit = jnp.where(pred(key_ref[pl.ds(off, ck), :], off), 1.0, 0.0)
            return acc + jnp.sum(hit.reshape(ck // COUNT_ROWS, COUNT_ROWS, nq), axis=0)
        acc = lax.fori_loop(0, nch, body, jnp.zeros((COUNT_ROWS, nq), F32))
        return jnp.sum(acc, axis=0, keepdims=True)

    tau = jnp.full((1, nq), INT_MIN, I32)
    n_ge = (i * nq + lax.broadcasted_iota(I32, (1, nq), 1) + 1).astype(F32)
    for bit in range(31, -1, -1):
        cand = jnp.zeros((1, nq), I32) if bit == 31 else tau | (1 << bit)
        cnt = count(lambda kk, off, cand=cand: kk >= cand)
        ok = cnt >= K
        tau = jnp.where(ok, cand, tau)
        n_ge = jnp.where(ok, cnt, n_ge)
    tau = jnp.maximum(tau, INT_MIN + 1)
    j_ref[...] = jnp.full((1, nq), 2 ** 30, I32)

    @pl.when(jnp.max(n_ge) > K)
    def _():
        need = K - count(lambda kk, off: kk > tau)
        jj = jnp.zeros((1, nq), I32)
        for bit in range(T.bit_length() - 1, -1, -1):
            cand = jj | (1 << bit)
            f = count(lambda kk, off, cand=cand: jnp.where(kk == tau, off + row, 2 ** 30) < cand)
            jj = jnp.where(f <= need, cand, jj)
        j_ref[...] = jj

    jlim = j_ref[...]

    def write_body(c, carry):
        off = pl.multiple_of(c * ck, ck)
        kk = key_ref[pl.ds(off, ck), :]
        tie_pos = jnp.where(kk == tau, off + row, 2 ** 30)
        bias = jnp.where(kk > tau, 0.0, jnp.where(tie_pos < jlim, 0.0, MASK_BIAS))
        m_ref[pl.ds(off, ck), :] = bias.astype(BF16)
        return carry

    lax.fori_loop(0, nch, write_body, 0)

    def fill_body(c, carry):
        off = pl.multiple_of(c * ck, ck)
        m_ref[pl.ds(off, ck), :] = jnp.full((ck, nq), MASK_BIAS, BF16)
        return carry

    lax.fori_loop(nch, T // ck, fill_body, 0)


def _select_prompt(ik3, iqth, iqtl, iwt, B, T, ck):
    nq = T // LANES
    K = min(TOPK_MAX, T // 4)
    return pl.pallas_call(
        functools.partial(_sel_kernel, T=T, K=K, ck=ck),
        out_shape=jax.ShapeDtypeStruct((B, T, T), BF16),
        grid=(B, nq),
        in_specs=[pl.BlockSpec((T, 256), lambda b, i: (b, 0)),
                  pl.BlockSpec((512, LANES), lambda b, i: (0, b * nq + i)),
                  pl.BlockSpec((512, LANES), lambda b, i: (0, b * nq + i)),
                  pl.BlockSpec((IDX_HEADS, LANES), lambda b, i: (0, b * nq + i))],
        out_specs=pl.BlockSpec((None, T, LANES), lambda b, i: (b, 0, i)),
        scratch_shapes=[pltpu.VMEM((T, LANES), I32), pltpu.VMEM((256, IDX_HEADS * LANES), BF16),
                        pltpu.VMEM((1, LANES), I32)],
        compiler_params=_cp("parallel", "arbitrary"),
        name="dsa_select_prompt",
    )(ik3, iqth, iqtl, iwt)


def _att_kernel(q_ref, k_ref, vt_ref, m_ref, o_ref, p_ref, s_ref, *, ck, big):
    i = pl.program_id(2)
    nq = LANES
    n_it = ((i + 1) * nq + big - 1) // big
    q2 = q_ref[...]
    qs = jnp.concatenate([q2[:, :HEAD_DIM], q2[:, HEAD_DIM:]], axis=0)

    def scores(c):
        off = pl.multiple_of(c * big, big)
        bias = m_ref[pl.ds(off, big), :].astype(F32)
        return off, _dot_nt(k_ref[pl.ds(off, big), :], qs) + jnp.concatenate([bias, bias], axis=1)

    def softmax_step(sm, m, l):
        m_new = jnp.maximum(m, jnp.max(sm, axis=0, keepdims=True))
        alpha = jnp.exp(m - m_new)
        p = jnp.exp(sm - m_new)
        return m_new, alpha, l * alpha + jnp.sum(p, axis=0, keepdims=True), p.astype(BF16)

    def pv(off, slot):
        out = _dot(vt_ref[off // ck], p_ref[slot, 0:ck, :])
        for u in range(1, big // ck):
            out = out + _dot(vt_ref[off // ck + u], p_ref[slot, u * ck:(u + 1) * ck, :])
        return out

    off0, sm = scores(0)
    m, alpha, l, pb = softmax_step(sm, jnp.full((1, 2 * nq), M_INIT, F32), jnp.zeros((1, 2 * nq), F32))
    p_ref[0] = pb
    off1, sm = scores(jnp.minimum(1, n_it - 1))
    s_ref[1] = sm

    def body(c, carry):
        m, l, acc, alpha_prev, off_prev, off_cur = carry
        pending = pv(off_prev, (c - 1) % 2)
        m, alpha, l, pb = softmax_step(s_ref[c % 2], m, l)
        p_ref[c % 2] = pb
        off_next, sm_next = scores(jnp.minimum(c + 1, n_it - 1))
        s_ref[(c + 1) % 2] = sm_next
        return m, l, acc * alpha_prev + pending, alpha, off_cur, off_next

    _, l, acc, alpha_prev, off_prev, _ = lax.fori_loop(
        1, n_it, body, (m, l, jnp.zeros((HEAD_DIM, 2 * nq), F32), alpha, off0, off1))
    ot = (acc * alpha_prev + pv(off_prev, (n_it - 1) % 2)) / l
    o_ref[:, 0:HEAD_DIM] = ot[:, 0:nq].T.astype(BF16)
    o_ref[:, HEAD_DIM:2 * HEAD_DIM] = ot[:, nq:2 * nq].T.astype(BF16)


def _attend_prompt(qa, kb, vt3, mask, B, T, ck):
    nq = T // LANES
    M = B * T
    big = min(2 * ck, T)
    return pl.pallas_call(
        functools.partial(_att_kernel, ck=ck, big=big),
        out_shape=jax.ShapeDtypeStruct((M, ATT_HEADS * HEAD_DIM), BF16),
        grid=(B, KV_HEADS, nq),
        in_specs=[pl.BlockSpec((LANES, 2 * HEAD_DIM), lambda b, g, i: (b * nq + i, g)),
                  pl.BlockSpec((T, HEAD_DIM), lambda b, g, i: (b, g)),
                  pl.BlockSpec((T // ck, HEAD_DIM, ck), lambda b, g, i: (b, g, 0)),
                  pl.BlockSpec((None, T, LANES), lambda b, g, i: (b, 0, i))],
        out_specs=pl.BlockSpec((LANES, 2 * HEAD_DIM), lambda b, g, i: (b * nq + i, g)),
        scratch_shapes=[pltpu.VMEM((2, big, 2 * LANES), BF16), pltpu.VMEM((2, big, 2 * LANES), F32)],
        compiler_params=_cp("parallel", "parallel", "arbitrary"),
        name="dsa_attend_prompt",
    )(qa, kb, vt3, mask)


def _score_step_kernel(pt_ref, iqt_ref, iw_ref, *refs, pps):
    ik_refs = refs[:pps]
    sc_ref = refs[pps]
    p = pl.program_id(1)
    iqt = iqt_ref[...]
    iw = iw_ref[...]
    mult = [jnp.broadcast_to(iqt[:, h:h + 1], (IDX_DIM, PAGE)) for h in range(IDX_HEADS)]
    for j in range(pps):
        pg = ik_refs[j][...]
        s = jnp.zeros((1, PAGE), F32)
        for h in range(IDX_HEADS):
            d = jnp.sum(pg * mult[h], axis=0, keepdims=True)
            s = s + jnp.maximum(d, 0.0) * iw[h:h + 1, :]
        sc_ref[pl.ds(p * pps + j, 1), :] = s


def _threshold_step_kernel(sc_ref, iq_ref, iw_ref, iko_ref, msk_ref, own_ref, *, K):
    key = _sortable(sc_ref[...])
    n_pages = key.shape[1]
    iw = iw_ref[...]
    d_own = jnp.sum(iq_ref[...] * iko_ref[...], axis=2, keepdims=True)
    s_own = jnp.sum(jnp.maximum(d_own, 0.0) * iw, axis=1, keepdims=True)
    key_own = _sortable(s_own)
    pos = lax.broadcasted_iota(I32, key.shape, 1) * PAGE + lax.broadcasted_iota(I32, key.shape, 2)
    pos_own = n_pages * PAGE

    def count(pred):
        c = jnp.sum(jnp.where(pred(key, pos), 1.0, 0.0), axis=1, keepdims=True)
        c = jnp.sum(c, axis=2, keepdims=True)
        return c + jnp.where(pred(key_own, pos_own), 1.0, 0.0)

    tau = jnp.full(key_own.shape, INT_MIN, I32)
    for bit in range(31, -1, -1):
        cand = jnp.zeros(key_own.shape, I32) if bit == 31 else tau | (1 << bit)
        tau = jnp.where(count(lambda kk, ps, cand=cand: kk >= cand) >= K, cand, tau)
    tau = jnp.maximum(tau, INT_MIN + 1)
    need = K - count(lambda kk, ps: kk > tau)
    jj = jnp.zeros(key_own.shape, I32)
    for bit in range((pos_own + 1).bit_length() - 1, -1, -1):
        cand = jj | (1 << bit)
        f = count(lambda kk, ps, cand=cand: jnp.where(kk == tau, ps, 2 ** 30) < cand)
        jj = jnp.where(f <= need, cand, jj)

    def selected(kk, ps):
        return jnp.where(kk > tau, 1.0, jnp.where(jnp.where(kk == tau, ps, 2 ** 30) < jj, 1.0, 0.0))

    msk_ref[...] = selected(key, pos)
    own_ref[...] = jnp.broadcast_to(selected(key_own, pos_own), own_ref.shape)


def _select_sample(page_table, iq8, iw8, ik_own, cache_idx_t, layer, pps):
    Bs, n_pages = page_table.shape
    K = min(TOPK_MAX, (n_pages * PAGE + 1) // 4)
    page_spec = lambda j: pl.BlockSpec((None, None, IDX_DIM, PAGE),
                                       lambda b, p, pt, j=j: (layer, pt[b, p * pps + j], 0, 0))
    grid_spec = pltpu.PrefetchScalarGridSpec(
        num_scalar_prefetch=1,
        grid=(Bs, n_pages // pps),
        in_specs=[pl.BlockSpec((None, IDX_DIM, IDX_HEADS), lambda b, p, pt: (b, 0, 0)),
                  pl.BlockSpec((None, IDX_HEADS, 1), lambda b, p, pt: (b, 0, 0))]
                 + [page_spec(j) for j in range(pps)],
        out_specs=pl.BlockSpec((None, n_pages, PAGE), lambda b, p, pt: (b, 0, 0)),
    )
    scores = pl.pallas_call(
        functools.partial(_score_step_kernel, pps=pps),
        out_shape=jax.ShapeDtypeStruct((Bs, n_pages, PAGE), F32),
        grid_spec=grid_spec,
        compiler_params=_cp("parallel", "arbitrary"),
        name="dsa_score_sample",
    )(page_table, jnp.swapaxes(iq8, 1, 2), iw8, *([cache_idx_t] * pps))
    whole = lambda shape: pl.BlockSpec(shape, lambda i: (0,) * len(shape))
    return pl.pallas_call(
        functools.partial(_threshold_step_kernel, K=K),
        out_shape=[jax.ShapeDtypeStruct((Bs, n_pages, PAGE), F32), jax.ShapeDtypeStruct((Bs, 1, LANES), F32)],
        grid=(1,),
        in_specs=[whole((Bs, n_pages, PAGE)), whole((Bs, IDX_HEADS, IDX_DIM)), whole((Bs, IDX_HEADS, 1)),
                  whole((Bs, 1, IDX_DIM))],
        out_specs=[whole((Bs, n_pages, PAGE)), whole((Bs, 1, LANES))],
        compiler_params=_cp("arbitrary"),
        name="dsa_threshold_sample",
    )(scores, iq8, iw8, ik_own)


def _att_step_kernel(pt_ref, q_ref, msk_ref, own_ref, ko_ref, vo_ref, *refs, pp):
    k_refs, v_refs = refs[:pp], refs[pp:2 * pp]
    o_ref, m_sc, l_sc, acc_sc = refs[2 * pp:]
    p = pl.program_id(1)
    rows = PAGE * KV_HEADS
    grp = ATT_HEADS // KV_HEADS

    @pl.when(p == 0)
    def _():
        m_sc[...] = jnp.full(m_sc.shape, NEG, F32)
        l_sc[...] = jnp.zeros(l_sc.shape, F32)
        acc_sc[...] = jnp.zeros(acc_sc.shape, F32)

    q8 = q_ref[...]
    own_kv = (lax.broadcasted_iota(I32, (ATT_HEADS, rows), 1) % KV_HEADS
              == lax.broadcasted_iota(I32, (ATT_HEADS, rows), 0) // grp)
    expand = jnp.where(lax.broadcasted_iota(I32, (PAGE, rows), 1) // KV_HEADS
                       == lax.broadcasted_iota(I32, (PAGE, rows), 0), 1.0, 0.0).astype(BF16)
    sms = []
    for j in range(pp):
        s = _dot_nt(q8, k_refs[j][...].astype(BF16))
        mrow = jnp.broadcast_to(msk_ref[pl.ds(p * pp + j, 1), :], (ATT_HEADS, PAGE)).astype(BF16)
        picked = _dot(mrow, expand)
        sms.append(jnp.where(jnp.where(own_kv, picked, 0.0) > 0.0, s, NEG))
    m = m_sc[...]
    m_new = m
    for sm in sms:
        m_new = jnp.maximum(m_new, jnp.max(sm, axis=1, keepdims=True))
    alpha = jnp.exp(m - m_new)
    l = l_sc[...] * alpha
    acc = acc_sc[...] * alpha
    for j in range(pp):
        pr = jnp.where(sms[j] > 0.5 * NEG, jnp.exp(sms[j] - m_new), 0.0)
        l = l + jnp.sum(pr, axis=1, keepdims=True)
        acc = acc + _dot(pr.astype(BF16), v_refs[j][...].astype(BF16))
    m_sc[...], l_sc[...], acc_sc[...] = m_new, l, acc

    @pl.when(p == pl.num_programs(1) - 1)
    def _():
        ko = ko_ref[...].astype(BF16).astype(F32)
        s_own = jnp.sum(q8.astype(F32) * ko, axis=1, keepdims=True)
        sel_own = own_ref[:, 0:1] > 0.0
        sm = jnp.where(sel_own, s_own, NEG)
        m_fin = jnp.maximum(m_new, sm)
        a_fin = jnp.exp(m_new - m_fin)
        pr = jnp.where(sel_own, jnp.exp(sm - m_fin), 0.0)
        o = (acc * a_fin + pr * vo_ref[...]) / (l * a_fin + pr)
        o_ref[...] = o.astype(BF16)


def _attend_sample(page_table, q8, msk, own, k_own8, v_own8, cache_k, cache_v, layer, pp):
    Bs, n_pages = page_table.shape
    rows = PAGE * KV_HEADS
    page_spec = lambda j: pl.BlockSpec((None, None, rows, HEAD_DIM),
                                       lambda b, p, pt, j=j: (layer, pt[b, p * pp + j], 0, 0))
    per_b = lambda r, w: pl.BlockSpec((None, r, w), lambda b, p, pt: (b, 0, 0))
    grid_spec = pltpu.PrefetchScalarGridSpec(
        num_scalar_prefetch=1,
        grid=(Bs, n_pages // pp),
        in_specs=[per_b(ATT_HEADS, HEAD_DIM), per_b(n_pages, PAGE), per_b(1, LANES),
                  per_b(ATT_HEADS, HEAD_DIM), per_b(ATT_HEADS, HEAD_DIM)]
                 + [page_spec(j) for j in range(pp)] + [page_spec(j) for j in range(pp)],
        out_specs=per_b(ATT_HEADS, HEAD_DIM),
        scratch_shapes=[pltpu.VMEM((ATT_HEADS, 1), F32), pltpu.VMEM((ATT_HEADS, 1), F32),
                        pltpu.VMEM((ATT_HEADS, HEAD_DIM), F32)],
    )
    return pl.pallas_call(
        functools.partial(_att_step_kernel, pp=pp),
        out_shape=jax.ShapeDtypeStruct((Bs, ATT_HEADS, HEAD_DIM), BF16),
        grid_spec=grid_spec,
        compiler_params=_cp("parallel", "arbitrary"),
        name="dsa_attend_sample",
    )(page_table, q8, msk, own, k_own8, v_own8, *([cache_k] * pp), *([cache_v] * pp))


def _mixout_kernel(x_ref, ro_ref, ca_ref, ao_ref, g0_ref, g1_ref, g2_ref, gm_ref,
                   wr_ref, wc_ref, wa_ref, wo_ref, o_ref):
    merged = (jax.nn.sigmoid(g0_ref[...]) * _dot(ro_ref[...], wr_ref[...])
              + jax.nn.sigmoid(g1_ref[...]) * _dot(ca_ref[...], wc_ref[...])
              + jax.nn.sigmoid(g2_ref[...]) * _dot(ao_ref[...], wa_ref[...]))
    o_ref[...] = x_ref[...] + gm_ref[0] * _dot(merged.astype(BF16), wo_ref[...])


def _mixout(x, ro, ca, ao, z, gate, w_r, w_c, w_a, w_o, tm, rpg):
    M, D = x.shape
    row = pl.BlockSpec((tm, D), lambda i: (i, 0))
    glc = lambda k: pl.BlockSpec((tm, D), lambda i, o=Z_GL // D + k: (i, o))
    wsp = pl.BlockSpec((D, D), lambda i: (0, 0))
    return pl.pallas_call(
        _mixout_kernel,
        out_shape=jax.ShapeDtypeStruct((M, D), F32),
        grid=(M // tm,),
        in_specs=[row, row, row, row, glc(0), glc(1), glc(2), _mod_spec(gate, tm, rpg), wsp, wsp, wsp, wsp],
        out_specs=row,
        compiler_params=_cp("parallel"),
        name="mixer_out",
    )(x, ro, ca, ao, z, z, z, gate, w_r, w_c, w_a, w_o)


def _rope_tables(pos, rot_dim, theta, period):
    half = rot_dim // 2
    inv = 1.0 / (theta ** (jnp.arange(half, dtype=F32) / half))
    ang = pos.astype(F32)[:, None] * inv[None, :]
    lane = np.arange(LANES) % period
    idx = lane % half
    cos = jnp.where(lane < rot_dim, jnp.cos(ang)[:, idx], 1.0)
    sin = jnp.sin(ang)[:, idx]
    s_lo = jnp.where(lane < half, -sin, 0.0)
    s_hi = jnp.where((lane >= half) & (lane < rot_dim), sin, 0.0)
    return cos, s_lo, s_hi


def _tables(pos):
    ca = _rope_tables(pos, ROT_DIM, ROPE_THETA, HEAD_DIM)
    ci = _rope_tables(pos, IDX_ROT, ROPE_THETA, IDX_DIM)
    cr, r_lo, r_hi = _rope_tables(pos, RET_DK, RET_THETA, RET_DK)
    return ca, ci, (cr, r_lo + r_hi)


def _mods(mod, rows):
    parts = jnp.split(mod, 9, axis=-1)
    if rows == 1:
        return [p[:, None, :] for p in parts]
    return [p[None] for p in parts]


def _cat_w_in(w):
    pad = Z_END - Z_IX - (IDX_DIM + IDX_HEADS)
    return jnp.concatenate([w[:, :6144], w[:, 7752:10824], w[:, 6144:7680], w[:, 7680:7752],
                            jnp.zeros((w.shape[0], pad), w.dtype)], axis=1).astype(BF16)


def kernel(x_prompt, x_sample, cache_k, cache_v, cache_idx_k, state_ret, state_conv, page_table, c_prompt, c_sample, w_ada, b_ada, norm_g, w_ff1_up, w_ff1_down, w_ff2_up, w_ff2_down, w_in, conv_w, conv_b, conv_g, q_norm_g, k_norm_g, idx_k_norm_g, w_ret_o, w_conv_o, w_att_o, w_o):
    B, T, D = x_prompt.shape
    Bs, Ts, _ = x_sample.shape
    assert Ts == 1 and T % 512 == 0 and D == 1024
    depth = w_ada.shape[0]
    n_pages = page_table.shape[1]
    past = n_pages * PAGE
    tm_p = 512
    ck = 256
    pps = 16 if n_pages % 16 == 0 else 4
    pp = 16 if n_pages % 16 == 0 else 4
    grp = ATT_HEADS // KV_HEADS

    tabs_p = _tables(jnp.arange(T, dtype=jnp.int32))
    tabs_s = _tables(jnp.full((Bs,), past, dtype=jnp.int32))
    lg = jnp.broadcast_to(jnp.log(1.0 - 2.0 ** (-5.0 - jnp.arange(RET_HEADS, dtype=F32)))[:, None],
                          (RET_HEADS, LANES))
    c_all = jnp.concatenate([c_prompt, c_sample], axis=0)
    ck4 = cache_k.reshape(cache_k.shape[0], cache_k.shape[1], PAGE * KV_HEADS, HEAD_DIM)
    cv4 = cache_v.reshape(cache_v.shape[0], cache_v.shape[1], PAGE * KV_HEADS, HEAD_DIM)
    cache_idx_t = jnp.swapaxes(cache_idx_k, 2, 3)
    conv_b3 = conv_b.reshape(depth, 1, D)
    conv_g3 = conv_g.reshape(depth, 1, D)

    xp = x_prompt.reshape(B * T, D)
    xs = x_sample.reshape(Bs, D)
    outs_p, outs_s = [], []
    for l in range(depth):
        mod = _ada_mod(c_all, w_ada, b_ada, l)
        mp = _mods(mod[:B], 1)
        msm = _mods(mod[B:], Bs)
        wu1, wd1 = w_ff1_up[l].astype(BF16), w_ff1_down[l].astype(BF16)
        wu2, wd2 = w_ff2_up[l].astype(BF16), w_ff2_down[l].astype(BF16)
        w_cat = _cat_w_in(w_in[l])
        w_r, w_c = w_ret_o[l].astype(BF16), w_conv_o[l].astype(BF16)
        w_a, w_oo = w_att_o[l].astype(BF16), w_o[l].astype(BF16)
        ng = norm_g[l]
        qg, kg = q_norm_g[l][None, :], k_norm_g[l][None, :]
        ikg = jnp.concatenate([idx_k_norm_g[l], jnp.zeros((LANES - IDX_DIM,), F32)])[None, :]

        xp = _ffn(xp, mp[0], mp[1], mp[2], ng[0:1], wu1, wd1, tm_p, T)
        z = _inproj(xp, mp[3], mp[4], ng[1:2], w_cat, min(1024, T), T)
        (qa, k_p, kb, v_p, _iqr, _ixo, idxk_p, vt3, iqth, iqtl, ik3, iwt) = _prep(
            z, tabs_p[0], tabs_p[1], qg, kg, ikg, tm_p, True, ck)
        ro, ret_p = _retention_prompt(z.reshape(B, T, Z_END), tabs_p[2][0], tabs_p[2][1], lg)
        ca, conv_p = _conv_prompt(z, B, T, conv_w, conv_b3, conv_g3, l, tm_p)
        mask = _select_prompt(ik3, iqth, iqtl, iwt, B, T, 2 * ck)
        ao = _attend_prompt(qa, kb, vt3, mask, B, T, ck)
        xp = _mixout(xp, ro.reshape(B * T, D), ca, ao, z, mp[5], w_r, w_c, w_a, w_oo, tm_p, T)
        xp = _ffn(xp, mp[6], mp[7], mp[8], ng[2:3], wu2, wd2, tm_p, T)
        outs_p.append((k_p.reshape(B, T, KV_HEADS, HEAD_DIM), v_p.reshape(B, T, KV_HEADS, HEAD_DIM),
                       idxk_p.reshape(B, T, IDX_DIM), ret_p, conv_p))

        xs = _ffn(xs, msm[0], msm[1], msm[2], ng[0:1], wu1, wd1, Bs, Bs)
        zs = _inproj(xs, msm[3], msm[4], ng[1:2], w_cat, Bs, Bs)
        qa_s, k_s, _kb, v_s, iqr_s, ixo_s, idxk_s = _prep(
            zs, tabs_s[0], tabs_s[1], qg, kg, ikg, Bs, False, ck)
        zs3 = zs.reshape(Bs, 1, Z_END)
        ro_s, ret_s = _retention_sample(zs3, state_ret, l, tabs_s[2][0][0:1], tabs_s[2][1][0:1], lg)
        ca_s, conv_s = _conv_sample(zs3, state_conv, conv_w, conv_b3, conv_g3, l)
        iq8 = iqr_s.reshape(Bs, IDX_HEADS, IDX_DIM)
        iw8 = ixo_s[:, IDX_DIM:IDX_DIM + IDX_HEADS].reshape(Bs, IDX_HEADS, 1)
        msk, own = _select_sample(page_table, iq8, iw8, idxk_s.reshape(Bs, 1, IDX_DIM), cache_idx_t, l, pps)
        k_own8 = jnp.repeat(k_s.reshape(Bs, KV_HEADS, HEAD_DIM), grp, axis=1)
        v_own8 = jnp.repeat(v_s.reshape(Bs, KV_HEADS, HEAD_DIM), grp, axis=1)
        ao_s = _attend_sample(page_table, qa_s.reshape(Bs, ATT_HEADS, HEAD_DIM), msk, own,
                              k_own8, v_own8, ck4, cv4, l, pp)
        xs = _mixout(xs, ro_s.reshape(Bs, D), ca_s.reshape(Bs, D), ao_s.reshape(Bs, D), zs, msm[5],
                     w_r, w_c, w_a, w_oo, Bs, Bs)
        xs = _ffn(xs, msm[6], msm[7], msm[8], ng[2:3], wu2, wd2, Bs, Bs)
        outs_s.append((k_s.reshape(Bs, 1, KV_HEADS, HEAD_DIM), v_s.reshape(Bs, 1, KV_HEADS, HEAD_DIM),
                       idxk_s.reshape(Bs, 1, IDX_DIM), ret_s, conv_s))

    stack = lambda outs, j: jnp.stack([o[j] for o in outs])
    return (xp.reshape(B, T, D), xs.reshape(Bs, 1, D),
            stack(outs_p, 0), stack(outs_p, 1), stack(outs_p, 2), stack(outs_p, 3), stack(outs_p, 4),
            stack(outs_s, 0), stack(outs_s, 1), stack(outs_s, 2), stack(outs_s, 3), stack(outs_s, 4))
```

```python
import functools

import numpy as np
import jax
import jax.numpy as jnp
from jax import lax
from jax.experimental import pallas as pl
from jax.experimental.pallas import tpu as pltpu

F32 = jnp.float32
BF16 = jnp.bfloat16
I32 = jnp.int32

EPS = 1e-6
LANES = 128
SUBLANES = 8
NEG = -1e30
MASK_BIAS = -1e30
M_INIT = -1e20
INT_MIN = -2 ** 31
COUNT_ROWS = 64
RET_HEADS = 4
RET_DK = 128
RET_DV = 256
RET_CHUNK = 128
RET_THETA = 10000.0
CONV_W = 31
ATT_HEADS = 8
KV_HEADS = 4
HEAD_DIM = 128
IDX_HEADS = 8
IDX_DIM = 64
TOPK_MAX = 256
ROPE_THETA = 500000.0
ROT_DIM = HEAD_DIM // 4
IDX_ROT = IDX_DIM // 4
PAGE = 128

Z_RQ, Z_RK, Z_RV, Z_RG = 0, 512, 1024, 2048
Z_CU, Z_CG, Z_AQ, Z_GL = 3072, 4096, 5120, 6144
Z_AK, Z_AV, Z_IQ, Z_IX, Z_END = 9216, 9728, 10240, 10752, 10880

VMEM_LIMIT = 56 * 1024 * 1024


def _cp(*sem):
    return pltpu.CompilerParams(dimension_semantics=tuple(sem), vmem_limit_bytes=VMEM_LIMIT)


def _dot(a, b):
    return jnp.dot(a, b, preferred_element_type=F32)


def _dot_nt(a, b):
    return lax.dot_general(a, b, (((1,), (1,)), ((), ())), preferred_element_type=F32)


def _split(x):
    hi = x.astype(BF16)
    lo = (x - hi.astype(F32)).astype(BF16)
    return hi, lo


def _dot3(a, b, nt=False):
    d = _dot_nt if nt else _dot
    ah, al = _split(a)
    bh, bl = _split(b)
    return d(ah, bh) + d(ah, bl) + d(al, bh)


def _silu(x):
    return x * jax.nn.sigmoid(x)


def _rms(x):
    return x * lax.rsqrt(jnp.mean(x * x, axis=-1, keepdims=True) + EPS)


def _norm_mod(x, g, shift, scale):
    return _rms(x) * g * (1.0 + scale) + shift


def _rot(x, cos, s_lo, s_hi, half):
    return x * cos + pltpu.roll(x, LANES - half, 1) * s_lo + pltpu.roll(x, half, 1) * s_hi


def _sortable(s):
    s = jnp.where(s == 0.0, 0.0, s)
    bits = pltpu.bitcast(s, I32)
    return bits ^ ((bits >> 31) & 0x7FFFFFFF)


def _ada_kernel(c_ref, w_ref, b_ref, o_ref):
    o_ref[...] = _dot3(_silu(c_ref[...]), w_ref[...]) + b_ref[...]


def _ada_mod(c, w_ada, b_ada, layer):
    R, D = c.shape
    N = w_ada.shape[2]
    tn = 1024
    return pl.pallas_call(
        _ada_kernel,
        out_shape=jax.ShapeDtypeStruct((R, N), F32),
        grid=(N // tn,),
        in_specs=[pl.BlockSpec((R, D), lambda j: (0, 0)),
                  pl.BlockSpec((None, D, tn), lambda j: (layer, 0, j)),
                  pl.BlockSpec((None, 1, tn), lambda j: (layer, 0, j))],
        out_specs=pl.BlockSpec((R, tn), lambda j: (0, j)),
        compiler_params=_cp("parallel"),
        name="ada_mod",
    )(c, w_ada, b_ada.reshape(b_ada.shape[0], 1, N))


def _ffn_kernel(x_ref, sh_ref, sc_ref, g_ref, ng_ref, wu_ref, wd_ref, o_ref, acc_ref, *, ff, tf):
    x = x_ref[...]
    h = _norm_mod(x, ng_ref[...], sh_ref[0], sc_ref[0]).astype(BF16)
    for j in range(ff // tf):
        a = _dot(h, wu_ref[:, j * tf:(j + 1) * tf])
        b = _dot(h, wu_ref[:, ff + j * tf:ff + (j + 1) * tf])
        act = (_silu(a) * b).astype(BF16)
        upd = _dot(act, wd_ref[j * tf:(j + 1) * tf, :])
        if j == 0:
            acc_ref[...] = upd
        else:
            acc_ref[...] += upd
    o_ref[...] = x + 0.5 * g_ref[0] * acc_ref[...]


def _mod_spec(mod, tm, rpg):
    rm = mod.shape[1]
    return pl.BlockSpec((1, rm, mod.shape[2]), lambda i: ((i * tm) // rpg, 0, 0))


def _ffn(x, shift, scale, gate, ng, w_up, w_down, tm, rpg):
    M, D = x.shape
    ff = w_down.shape[0]
    const = lambda i: (0, 0)
    return pl.pallas_call(
        functools.partial(_ffn_kernel, ff=ff, tf=256),
        out_shape=jax.ShapeDtypeStruct((M, D), F32),
        grid=(M // tm,),
        in_specs=[pl.BlockSpec((tm, D), lambda i: (i, 0)),
                  _mod_spec(shift, tm, rpg), _mod_spec(scale, tm, rpg), _mod_spec(gate, tm, rpg),
                  pl.BlockSpec((1, D), const),
                  pl.BlockSpec((D, 2 * ff), const),
                  pl.BlockSpec((ff, D), const)],
        out_specs=pl.BlockSpec((tm, D), lambda i: (i, 0)),
        scratch_shapes=[pltpu.VMEM((tm, D), F32)],
        compiler_params=_cp("parallel"),
        name="ffn",
    )(x, shift, scale, gate, ng, w_up, w_down)


def _inproj_kernel(x_ref, sh_ref, sc_ref, ng_ref, w_ref, o_ref, h_ref):
    @pl.when(pl.program_id(1) == 0)
    def _():
        h_ref[...] = _norm_mod(x_ref[...], ng_ref[...], sh_ref[0], sc_ref[0]).astype(BF16)

    o_ref[...] = _dot(h_ref[...], w_ref[...])


def _inproj(x, shift, scale, ng, w_cat, tm, rpg):
    M, D = x.shape
    N = w_cat.shape[1]
    tn = N // 5
    ms = lambda mod: pl.BlockSpec((1, mod.shape[1], D), lambda i, j: ((i * tm) // rpg, 0, 0))
    return pl.pallas_call(
        _inproj_kernel,
        out_shape=jax.ShapeDtypeStruct((M, N), F32),
        grid=(M // tm, N // tn),
        in_specs=[pl.BlockSpec((tm, D), lambda i, j: (i, 0)),
                  ms(shift), ms(scale),
                  pl.BlockSpec((1, D), lambda i, j: (0, 0)),
                  pl.BlockSpec((D, tn), lambda i, j: (0, j))],
        out_specs=pl.BlockSpec((tm, tn), lambda i, j: (i, j)),
        scratch_shapes=[pltpu.VMEM((tm, D), BF16)],
        compiler_params=_cp("parallel", "arbitrary"),
        name="inproj",
    )(x, shift, scale, ng, w_cat)


def _prep_kernel(aq_ref, ak_ref, av_ref, iq_ref, ix_ref,
                 ca_ref, sa1_ref, sa2_ref, ci_ref, si1_ref, si2_ref, qg_ref, kg_ref, ikg_ref,
                 qa_ref, k_ref, kb_ref, v_ref, iqr_ref, ixo_ref, idxk_ref, *t_refs, tck):
    ca, sa1, sa2 = ca_ref[...], sa1_ref[...], sa2_ref[...]
    ci, si1, si2 = ci_ref[...], si1_ref[...], si2_ref[...]
    qg, kg = qg_ref[...], kg_ref[...]
    q_scale = HEAD_DIM ** -0.5
    for h in range(ATT_HEADS):
        sl = slice(h * HEAD_DIM, (h + 1) * HEAD_DIM)
        y = _rot(_rms(aq_ref[:, sl]) * qg, ca, sa1, sa2, ROT_DIM // 2)
        qa_ref[:, sl] = (y * q_scale).astype(BF16)
    for h in range(KV_HEADS):
        sl = slice(h * HEAD_DIM, (h + 1) * HEAD_DIM)
        y = _rot(_rms(ak_ref[:, sl]) * kg, ca, sa1, sa2, ROT_DIM // 2)
        k_ref[:, sl] = y
        kb_ref[:, sl] = y.astype(BF16)
    av = av_ref[...]
    v_ref[...] = av
    for j in range(IDX_HEADS * IDX_DIM // LANES):
        sl = slice(j * LANES, (j + 1) * LANES)
        iqr_ref[:, sl] = _rot(iq_ref[:, sl], ci, si1, si2, IDX_ROT // 2)
    blk = ix_ref[...]
    lane = lax.broadcasted_iota(I32, blk.shape, 1)
    ikm = jnp.where(lane < IDX_DIM, blk, 0.0)
    ms = jnp.sum(ikm * ikm, axis=-1, keepdims=True) * (1.0 / IDX_DIM)
    ikr = _rot(ikm * lax.rsqrt(ms + EPS) * ikg_ref[...], ci, si1, si2, IDX_ROT // 2)
    iw_scale = IDX_HEADS ** -0.5 * IDX_DIM ** -0.5
    iws = jnp.where((lane >= IDX_DIM) & (lane < IDX_DIM + IDX_HEADS), blk * iw_scale, 0.0)
    ixo = ikr + iws
    ixo_ref[...] = ixo
    idxk_ref[...] = ikr[:, :IDX_DIM]
    if t_refs:
        vt_ref, iqth_ref, iqtl_ref, ik3_ref, iwt_ref = t_refs
        tm = av.shape[0]
        avt = av.T
        for c in range(tm // tck):
            vt_ref[c] = avt[:, c * tck:(c + 1) * tck].astype(BF16)
        iqt = iqr_ref[...].T
        hi = iqt.astype(BF16)
        iqth_ref[...] = hi
        iqtl_ref[...] = (iqt - hi.astype(F32)).astype(BF16)
        khi = ikr.astype(BF16).astype(F32)
        klo = ikr - khi
        ik3_ref[...] = jnp.concatenate([khi + pltpu.roll(klo, IDX_DIM, 1), khi], axis=1).astype(BF16)
        iwt_ref[...] = ixo.T[IDX_DIM:IDX_DIM + IDX_HEADS, :]


def _prep(z, tabs_a, tabs_i, qg, kg, ikg, tm, transposed, tck):
    M = z.shape[0]
    nt = M // tm
    tpb = tabs_a[0].shape[0] // tm
    tab = pl.BlockSpec((tm, LANES), lambda i: (i % tpb, 0))
    col = lambda w, off: pl.BlockSpec((tm, w), lambda i, o=off // w: (i, o))
    row = lambda w: pl.BlockSpec((tm, w), lambda i: (i, 0))
    vec = pl.BlockSpec((1, LANES), lambda i: (0, 0))
    in_specs = [col(1024, Z_AQ), col(512, Z_AK), col(512, Z_AV), col(512, Z_IQ), col(128, Z_IX)]
    in_specs += [tab] * 6 + [vec] * 3
    out_shape = [jax.ShapeDtypeStruct((M, 1024), BF16), jax.ShapeDtypeStruct((M, 512), F32),
                 jax.ShapeDtypeStruct((M, 512), BF16), jax.ShapeDtypeStruct((M, 512), F32),
                 jax.ShapeDtypeStruct((M, 512), F32), jax.ShapeDtypeStruct((M, LANES), F32),
                 jax.ShapeDtypeStruct((M, IDX_DIM), F32)]
    out_specs = [row(1024), row(512), row(512), row(512), row(512), row(LANES), row(IDX_DIM)]
    if transposed:
        out_shape += [jax.ShapeDtypeStruct((M // tck, 512, tck), BF16),
                      jax.ShapeDtypeStruct((512, M), BF16), jax.ShapeDtypeStruct((512, M), BF16),
                      jax.ShapeDtypeStruct((M, 256), BF16), jax.ShapeDtypeStruct((IDX_HEADS, M), F32)]
        out_specs += [pl.BlockSpec((tm // tck, 512, tck), lambda i: (i, 0, 0)),
                      pl.BlockSpec((512, tm), lambda i: (0, i)), pl.BlockSpec((512, tm), lambda i: (0, i)),
                      row(256), pl.BlockSpec((IDX_HEADS, tm), lambda i: (0, i))]
    return pl.pallas_call(
        functools.partial(_prep_kernel, tck=tck),
        out_shape=out_shape, grid=(nt,), in_specs=in_specs, out_specs=out_specs,
        compiler_params=_cp("parallel"), name="dsa_prep",
    )(z, z, z, z, z, *tabs_a, *tabs_i, qg, kg, ikg)


def _ret_kernel(q_ref, k_ref, v_ref, rg_ref, cos_ref, sin_ref, lg_ref, o_ref, st_ref, s_ref, *, nb):
    n = pl.program_id(0)
    C = RET_CHUNK

    @pl.when(n == 0)
    def _():
        s_ref[...] = jnp.zeros_like(s_ref)

    cos, sin = cos_ref[...], sin_ref[...]
    ri = lax.broadcasted_iota(I32, (C, C), 0).astype(F32)
    ci = lax.broadcasted_iota(I32, (C, C), 1).astype(F32)
    diff = ri - ci
    rv = lax.broadcasted_iota(I32, (C, RET_DV), 0).astype(F32)
    for h in range(RET_HEADS):
        lg = lg_ref[h:h + 1, :]
        lg2 = jnp.concatenate([lg, lg], axis=1)
        dmask = jnp.where(diff >= 0, jnp.exp(jnp.maximum(diff, 0.0) * lg), 0.0)
        cross = jnp.exp((rv + 1.0) * lg2)
        kdec = jnp.exp((C - 1.0 - ri) * lg)
        g_c = jnp.exp(C * lg2)
        for b in range(nb):
            qs = slice(h * RET_DK, (h + 1) * RET_DK)
            vs = slice(h * RET_DV, (h + 1) * RET_DV)
            q = q_ref[b, :, qs]
            k = k_ref[b, :, qs]
            v = v_ref[b, :, vs]
            qr = q * cos + pltpu.roll(q, RET_DK // 2, 1) * sin
            kr = (k * cos + pltpu.roll(k, RET_DK // 2, 1) * sin) * (RET_DK ** -0.5)
            s_old = s_ref[b * RET_HEADS + h]
            sc = _dot3(qr, kr, nt=True) * dmask
            out = _dot3(sc, v) + _dot3(qr, s_old) * cross
            s_ref[b * RET_HEADS + h] = g_c * s_old + _dot3((kr * kdec).T, v)
            rg = rg_ref[b, :, vs]
            o_ref[b, :, vs] = (_silu(rg) * _rms(out)).astype(BF16)

    @pl.when(n == pl.num_programs(0) - 1)
    def _():
        for b in range(nb):
            for h in range(RET_HEADS):
                st_ref[b, h] = s_ref[b * RET_HEADS + h]


def _retention_prompt(z3, cos_r, sin_r, lg):
    B, T, _ = z3.shape
    C = RET_CHUNK
    return pl.pallas_call(
        functools.partial(_ret_kernel, nb=B),
        out_shape=[jax.ShapeDtypeStruct((B, T, 1024), BF16),
                   jax.ShapeDtypeStruct((B, RET_HEADS, RET_DK, RET_DV), F32)],
        grid=(T // C,),
        in_specs=[pl.BlockSpec((B, C, 512), lambda n: (0, n, Z_RQ // 512)),
                  pl.BlockSpec((B, C, 512), lambda n: (0, n, Z_RK // 512)),
                  pl.BlockSpec((B, C, 1024), lambda n: (0, n, Z_RV // 1024)),
                  pl.BlockSpec((B, C, 1024), lambda n: (0, n, Z_RG // 1024)),
                  pl.BlockSpec((C, LANES), lambda n: (n, 0)),
                  pl.BlockSpec((C, LANES), lambda n: (n, 0)),
                  pl.BlockSpec((RET_HEADS, LANES), lambda n: (0, 0))],
        out_specs=[pl.BlockSpec((B, C, 1024), lambda n: (0, n, 0)),
                   pl.BlockSpec((B, RET_HEADS, RET_DK, RET_DV), lambda n: (0, 0, 0, 0))],
        scratch_shapes=[pltpu.VMEM((B * RET_HEADS, RET_DK, RET_DV), F32)],
        compiler_params=_cp("arbitrary"),
        name="retention_prompt",
    )(z3, z3, z3, z3, cos_r, sin_r, lg)


def _ret_step_kernel(q_ref, k_ref, v_ref, rg_ref, s0_ref, cos_ref, sin_ref, lg_ref, o_ref, st_ref):
    cos, sin = cos_ref[...], sin_ref[...]
    eye = (lax.broadcasted_iota(I32, (RET_DK, RET_DK), 0) == lax.broadcasted_iota(I32, (RET_DK, RET_DK), 1))
    for h in range(RET_HEADS):
        qs = slice(h * RET_DK, (h + 1) * RET_DK)
        vs = slice(h * RET_DV, (h + 1) * RET_DV)
        q, k, v = q_ref[:, qs], k_ref[:, qs], v_ref[:, vs]
        qr = q * cos + pltpu.roll(q, RET_DK // 2, 1) * sin
        kr = (k * cos + pltpu.roll(k, RET_DK // 2, 1) * sin) * (RET_DK ** -0.5)
        lg = lg_ref[h:h + 1, :]
        gamma = jnp.exp(jnp.concatenate([lg, lg], axis=1))
        qcol = jnp.sum(jnp.where(eye, qr, 0.0), axis=1, keepdims=True)
        kcol = jnp.sum(jnp.where(eye, kr, 0.0), axis=1, keepdims=True)
        s0 = s0_ref[h]
        qk = jnp.sum(qr * kr, axis=1, keepdims=True)
        out = qk * v + jnp.sum(qcol * s0, axis=0, keepdims=True) * gamma
        st_ref[h] = gamma * s0 + kcol * v
        o_ref[:, vs] = (_silu(rg_ref[:, vs]) * _rms(out)).astype(BF16)


def _retention_sample(z3, state_ret, layer, cos_r, sin_r, lg):
    Bs = z3.shape[0]
    zc = lambda w, off: pl.BlockSpec((None, 1, w), lambda b, o=off // w: (b, 0, o))
    return pl.pallas_call(
        _ret_step_kernel,
        out_shape=[jax.ShapeDtypeStruct((Bs, 1, 1024), BF16),
                   jax.ShapeDtypeStruct((Bs, RET_HEADS, RET_DK, RET_DV), F32)],
        grid=(Bs,),
        in_specs=[zc(512, Z_RQ), zc(512, Z_RK), zc(1024, Z_RV), zc(1024, Z_RG),
                  pl.BlockSpec((None, None, RET_HEADS, RET_DK, RET_DV), lambda b: (layer, b, 0, 0, 0)),
                  pl.BlockSpec((1, LANES), lambda b: (0, 0)),
                  pl.BlockSpec((1, LANES), lambda b: (0, 0)),
                  pl.BlockSpec((RET_HEADS, LANES), lambda b: (0, 0))],
        out_specs=[pl.BlockSpec((None, 1, 1024), lambda b: (b, 0, 0)),
                   pl.BlockSpec((None, RET_HEADS, RET_DK, RET_DV), lambda b: (b, 0, 0, 0))],
        compiler_params=_cp("parallel"),
        name="retention_sample",
    )(z3, z3, z3, z3, state_ret, cos_r, sin_r, lg)


CONV_HALO = 32
CONV_RB = 32
CONV_CB = 512


def _conv_kernel(a_ref, g_ref, w_ref, b_ref, cg_ref, o_ref, cs_ref, u_ref, sh_ref, y_ref, wb_ref):
    t = pl.program_id(1)
    tm = a_ref.shape[0]

    @pl.when(t == 0)
    def _():
        u_ref[0:CONV_HALO, :] = jnp.zeros((CONV_HALO, u_ref.shape[1]), F32)
        for j in range(CONV_W):
            wb_ref[j] = jnp.broadcast_to(w_ref[j:j + 1, :], (SUBLANES, w_ref.shape[1]))

    u_ref[CONV_HALO:CONV_HALO + tm, :] = a_ref[...] * jax.nn.sigmoid(g_ref[...])
    span = tm + CONV_HALO - SUBLANES
    for s in range(1, SUBLANES):
        sh_ref[s - 1, 0:span, :] = u_ref[s:s + span, :]
    first = CONV_HALO - (CONV_W - 1)
    for r in range(tm // CONV_RB):
        for c in range(u_ref.shape[1] // CONV_CB):
            cs = slice(c * CONV_CB, (c + 1) * CONV_CB)
            acc = jnp.zeros((CONV_RB // SUBLANES, SUBLANES, CONV_CB), F32)
            for j in range(CONV_W):
                s = (first + j) % SUBLANES
                lo = r * CONV_RB + first + j - s
                src = u_ref if s == 0 else sh_ref.at[s - 1]
                tap = src[lo:lo + CONV_RB, cs].reshape(CONV_RB // SUBLANES, SUBLANES, CONV_CB)
                acc = acc + wb_ref[j, :, cs][None] * tap
            y_ref[r * CONV_RB:(r + 1) * CONV_RB, cs] = acc.reshape(CONV_RB, CONV_CB)
    cy = y_ref[...] + b_ref[...]
    o_ref[...] = _silu(_rms(cy) * cg_ref[...]).astype(BF16)

    @pl.when(t == pl.num_programs(1) - 1)
    def _():
        cs_ref[...] = u_ref[tm + first:tm + CONV_HALO, :]

    u_ref[0:CONV_HALO, :] = u_ref[tm:tm + CONV_HALO, :]


def _conv_prompt(z, B, T, conv_w, conv_b, conv_g, layer, tm):
    M, C = B * T, 1024
    nt = T // tm
    return pl.pallas_call(
        _conv_kernel,
        out_shape=[jax.ShapeDtypeStruct((M, C), BF16), jax.ShapeDtypeStruct((B, CONV_W - 1, C), F32)],
        grid=(B, nt),
        in_specs=[pl.BlockSpec((tm, C), lambda b, t: (b * nt + t, Z_CU // C)),
                  pl.BlockSpec((tm, C), lambda b, t: (b * nt + t, Z_CG // C)),
                  pl.BlockSpec((None, CONV_W, C), lambda b, t: (layer, 0, 0)),
                  pl.BlockSpec((None, 1, C), lambda b, t: (layer, 0, 0)),
                  pl.BlockSpec((None, 1, C), lambda b, t: (layer, 0, 0))],
        out_specs=[pl.BlockSpec((tm, C), lambda b, t: (b * nt + t, 0)),
                   pl.BlockSpec((None, CONV_W - 1, C), lambda b, t: (b, 0, 0))],
        scratch_shapes=[pltpu.VMEM((tm + CONV_HALO, C), F32), pltpu.VMEM((SUBLANES - 1, tm + CONV_HALO, C), F32),
                        pltpu.VMEM((tm, C), F32), pltpu.VMEM((CONV_W, SUBLANES, C), F32)],
        compiler_params=_cp("parallel", "arbitrary"),
        name="conv_prompt",
    )(z, z, conv_w, conv_b, conv_g)


def _conv_step_kernel(a_ref, g_ref, buf_ref, w_ref, b_ref, cg_ref, o_ref, cs_ref):
    u = a_ref[...] * jax.nn.sigmoid(g_ref[...])
    buf = buf_ref[...]
    y = jnp.sum(w_ref[0:CONV_W - 1, :] * buf, axis=0, keepdims=True) + w_ref[CONV_W - 1:CONV_W, :] * u
    cy = y + b_ref[...]
    o_ref[...] = _silu(_rms(cy) * cg_ref[...]).astype(BF16)
    cs_ref[0:CONV_W - 2, :] = buf[1:CONV_W - 1, :]
    cs_ref[CONV_W - 2:CONV_W - 1, :] = u


def _conv_sample(z3, state_conv, conv_w, conv_b, conv_g, layer):
    Bs, C = z3.shape[0], 1024
    return pl.pallas_call(
        _conv_step_kernel,
        out_shape=[jax.ShapeDtypeStruct((Bs, 1, C), BF16), jax.ShapeDtypeStruct((Bs, CONV_W - 1, C), F32)],
        grid=(Bs,),
        in_specs=[pl.BlockSpec((None, 1, C), lambda b: (b, 0, Z_CU // C)),
                  pl.BlockSpec((None, 1, C), lambda b: (b, 0, Z_CG // C)),
                  pl.BlockSpec((None, None, CONV_W - 1, C), lambda b: (layer, b, 0, 0)),
                  pl.BlockSpec((None, CONV_W, C), lambda b: (layer, 0, 0)),
                  pl.BlockSpec((None, 1, C), lambda b: (layer, 0, 0)),
                  pl.BlockSpec((None, 1, C), lambda b: (layer, 0, 0))],
        out_specs=[pl.BlockSpec((None, 1, C), lambda b: (b, 0, 0)),
                   pl.BlockSpec((None, CONV_W - 1, C), lambda b: (b, 0, 0))],
        compiler_params=_cp("parallel"),
        name="conv_sample",
    )(z3, z3, state_conv, conv_w, conv_b, conv_g)


def _sel_kernel(ik3_ref, iqh_ref, iql_ref, iw_ref, m_ref, key_ref, w_ref, j_ref, *, T, K, ck):
    i = pl.program_id(1)
    nq = LANES
    for h in range(IDX_HEADS):
        rs = slice(h * IDX_DIM, (h + 1) * IDX_DIM)
        cs = slice(h * nq, (h + 1) * nq)
        hi = iqh_ref[rs, :]
        w_ref[0:IDX_DIM, cs] = hi
        w_ref[IDX_DIM:2 * IDX_DIM, cs] = hi
        w_ref[2 * IDX_DIM:3 * IDX_DIM, cs] = iql_ref[rs, :]
        w_ref[3 * IDX_DIM:4 * IDX_DIM, cs] = jnp.zeros((IDX_DIM, nq), BF16)
    nch = ((i + 1) * nq + ck - 1) // ck
    tq = i * nq + lax.broadcasted_iota(I32, (ck, nq), 1)
    row = lax.broadcasted_iota(I32, (ck, nq), 0)
    iw = iw_ref[...]

    def score_body(c, carry):
        off = pl.multiple_of(c * ck, ck)
        d = _dot(ik3_ref[pl.ds(off, ck), :], w_ref[...])
        s = jnp.zeros((ck, nq), F32)
        for h in range(IDX_HEADS):
            s = s + jnp.maximum(d[:, h * nq:(h + 1) * nq], 0.0) * iw[h:h + 1, :]
        key_ref[pl.ds(off, ck), :] = jnp.where(off + row <= tq, _sortable(s), INT_MIN)
        return carry

    lax.fori_loop(0, nch, score_body, 0)

    def count(pred):
        def body(c, acc):
            off = pl.multiple_of(c * ck, ck)
            hit = jnp.where(pred(key_ref[pl.ds(off, ck), :], off), 1.0, 0.0)
            return acc + jnp.sum(hit.reshape(ck // COUNT_ROWS, COUNT_ROWS, nq), axis=0)
        acc = lax.fori_loop(0, nch, body, jnp.zeros((COUNT_ROWS, nq), F32))
        return jnp.sum(acc, axis=0, keepdims=True)

    tau = jnp.full((1, nq), INT_MIN, I32)
    n_ge = (i * nq + lax.broadcasted_iota(I32, (1, nq), 1) + 1).astype(F32)
    for bit in range(31, -1, -1):
        cand = jnp.zeros((1, nq), I32) if bit == 31 else tau | (1 << bit)
        cnt = count(lambda kk, off, cand=cand: kk >= cand)
        ok = cnt >= K
        tau = jnp.where(ok, cand, tau)
        n_ge = jnp.where(ok, cnt, n_ge)
    tau = jnp.maximum(tau, INT_MIN + 1)
    j_ref[...] = jnp.full((1, nq), 2 ** 30, I32)

    @pl.when(jnp.max(n_ge) > K)
    def _():
        need = K - count(lambda kk, off: kk > tau)
        jj = jnp.zeros((1, nq), I32)
        for bit in range(T.bit_length() - 1, -1, -1):
            cand = jj | (1 << bit)
            f = count(lambda kk, off, cand=cand: jnp.where(kk == tau, off + row, 2 ** 30) < cand)
            jj = jnp.where(f <= need, cand, jj)
        j_ref[...] = jj

    jlim = j_ref[...]

    def write_body(c, carry):
        off = pl.multiple_of(c * ck, ck)
        kk = key_ref[pl.ds(off, ck), :]
        tie_pos = jnp.where(kk == tau, off + row, 2 ** 30)
        bias = jnp.where(kk > tau, 0.0, jnp.where(tie_pos < jlim, 0.0, MASK_BIAS))
        m_ref[pl.ds(off, ck), :] = bias.astype(BF16)
        return carry

    lax.fori_loop(0, nch, write_body, 0)

    def fill_body(c, carry):
        off = pl.multiple_of(c * ck, ck)
        m_ref[pl.ds(off, ck), :] = jnp.full((ck, nq), MASK_BIAS, BF16)
        return carry

    lax.fori_loop(nch, T // ck, fill_body, 0)


def _select_prompt(ik3, iqth, iqtl, iwt, B, T, ck):
    nq = T // LANES
    K = min(TOPK_MAX, T // 4)
    return pl.pallas_call(
        functools.partial(_sel_kernel, T=T, K=K, ck=ck),
        out_shape=jax.ShapeDtypeStruct((B, T, T), BF16),
        grid=(B, nq),
        in_specs=[pl.BlockSpec((T, 256), lambda b, i: (b, 0)),
                  pl.BlockSpec((512, LANES), lambda b, i: (0, b * nq + i)),
                  pl.BlockSpec((512, LANES), lambda b, i: (0, b * nq + i)),
                  pl.BlockSpec((IDX_HEADS, LANES), lambda b, i: (0, b * nq + i))],
        out_specs=pl.BlockSpec((None, T, LANES), lambda b, i: (b, 0, i)),
        scratch_shapes=[pltpu.VMEM((T, LANES), I32), pltpu.VMEM((256, IDX_HEADS * LANES), BF16),
                        pltpu.VMEM((1, LANES), I32)],
        compiler_params=_cp("parallel", "arbitrary"),
        name="dsa_select_prompt",
    )(ik3, iqth, iqtl, iwt)


def _att_kernel(q_ref, k_ref, vt_ref, m_ref, o_ref, p_ref, s_ref, *, ck, big):
    i = pl.program_id(2)
    nq = LANES
    n_it = ((i + 1) * nq + big - 1) // big
    q2 = q_ref[...]
    qs = jnp.concatenate([q2[:, :HEAD_DIM], q2[:, HEAD_DIM:]], axis=0)

    def scores(c):
        off = pl.multiple_of(c * big, big)
        bias = m_ref[pl.ds(off, big), :].astype(F32)
        return off, _dot_nt(k_ref[pl.ds(off, big), :], qs) + jnp.concatenate([bias, bias], axis=1)

    def softmax_step(sm, m, l):
        m_new = jnp.maximum(m, jnp.max(sm, axis=0, keepdims=True))
        alpha = jnp.exp(m - m_new)
        p = jnp.exp(sm - m_new)
        return m_new, alpha, l * alpha + jnp.sum(p, axis=0, keepdims=True), p.astype(BF16)

    def pv(off, slot):
        out = _dot(vt_ref[off // ck], p_ref[slot, 0:ck, :])
        for u in range(1, big // ck):
            out = out + _dot(vt_ref[off // ck + u], p_ref[slot, u * ck:(u + 1) * ck, :])
        return out

    off0, sm = scores(0)
    m, alpha, l, pb = softmax_step(sm, jnp.full((1, 2 * nq), M_INIT, F32), jnp.zeros((1, 2 * nq), F32))
    p_ref[0] = pb
    off1, sm = scores(jnp.minimum(1, n_it - 1))
    s_ref[1] = sm

    def body(c, carry):
        m, l, acc, alpha_prev, off_prev, off_cur = carry
        pending = pv(off_prev, (c - 1) % 2)
        m, alpha, l, pb = softmax_step(s_ref[c % 2], m, l)
        p_ref[c % 2] = pb
        off_next, sm_next = scores(jnp.minimum(c + 1, n_it - 1))
        s_ref[(c + 1) % 2] = sm_next
        return m, l, acc * alpha_prev + pending, alpha, off_cur, off_next

    _, l, acc, alpha_prev, off_prev, _ = lax.fori_loop(
        1, n_it, body, (m, l, jnp.zeros((HEAD_DIM, 2 * nq), F32), alpha, off0, off1))
    ot = (acc * alpha_prev + pv(off_prev, (n_it - 1) % 2)) / l
    o_ref[:, 0:HEAD_DIM] = ot[:, 0:nq].T.astype(BF16)
    o_ref[:, HEAD_DIM:2 * HEAD_DIM] = ot[:, nq:2 * nq].T.astype(BF16)


def _attend_prompt(qa, kb, vt3, mask, B, T, ck):
    nq = T // LANES
    M = B * T
    big = min(4 * ck, T)
    return pl.pallas_call(
        functools.partial(_att_kernel, ck=ck, big=big),
        out_shape=jax.ShapeDtypeStruct((M, ATT_HEADS * HEAD_DIM), BF16),
        grid=(B, KV_HEADS, nq),
        in_specs=[pl.BlockSpec((LANES, 2 * HEAD_DIM), lambda b, g, i: (b * nq + i, g)),
                  pl.BlockSpec((T, HEAD_DIM), lambda b, g, i: (b, g)),
                  pl.BlockSpec((T // ck, HEAD_DIM, ck), lambda b, g, i: (b, g, 0)),
                  pl.BlockSpec((None, T, LANES), lambda b, g, i: (b, 0, i))],
        out_specs=pl.BlockSpec((LANES, 2 * HEAD_DIM), lambda b, g, i: (b * nq + i, g)),
        scratch_shapes=[pltpu.VMEM((2, big, 2 * LANES), BF16), pltpu.VMEM((2, big, 2 * LANES), F32)],
        compiler_params=_cp("parallel", "parallel", "arbitrary"),
        name="dsa_attend_prompt",
    )(qa, kb, vt3, mask)


def _score_step_kernel(pt_ref, iqt_ref, iw_ref, *refs, pps):
    ik_refs = refs[:pps]
    sc_ref = refs[pps]
    p = pl.program_id(1)
    iqt = iqt_ref[...]
    iw = iw_ref[...]
    mult = [jnp.broadcast_to(iqt[:, h:h + 1], (IDX_DIM, PAGE)) for h in range(IDX_HEADS)]
    for j in range(pps):
        pg = ik_refs[j][...]
        s = jnp.zeros((1, PAGE), F32)
        for h in range(IDX_HEADS):
            d = jnp.sum(pg * mult[h], axis=0, keepdims=True)
            s = s + jnp.maximum(d, 0.0) * iw[h:h + 1, :]
        sc_ref[pl.ds(p * pps + j, 1), :] = s


def _threshold_step_kernel(sc_ref, iq_ref, iw_ref, iko_ref, msk_ref, own_ref, *, K):
    key = _sortable(sc_ref[...])
    n_pages = key.shape[1]
    iw = iw_ref[...]
    d_own = jnp.sum(iq_ref[...] * iko_ref[...], axis=2, keepdims=True)
    s_own = jnp.sum(jnp.maximum(d_own, 0.0) * iw, axis=1, keepdims=True)
    key_own = _sortable(s_own)
    pos = lax.broadcasted_iota(I32, key.shape, 1) * PAGE + lax.broadcasted_iota(I32, key.shape, 2)
    pos_own = n_pages * PAGE

    def count(pred):
        c = jnp.sum(jnp.where(pred(key, pos), 1.0, 0.0), axis=1, keepdims=True)
        c = jnp.sum(c, axis=2, keepdims=True)
        return c + jnp.where(pred(key_own, pos_own), 1.0, 0.0)

    tau = jnp.full(key_own.shape, INT_MIN, I32)
    for bit in range(31, -1, -1):
        cand = jnp.zeros(key_own.shape, I32) if bit == 31 else tau | (1 << bit)
        tau = jnp.where(count(lambda kk, ps, cand=cand: kk >= cand) >= K, cand, tau)
    tau = jnp.maximum(tau, INT_MIN + 1)
    need = K - count(lambda kk, ps: kk > tau)
    jj = jnp.zeros(key_own.shape, I32)
    for bit in range((pos_own + 1).bit_length() - 1, -1, -1):
        cand = jj | (1 << bit)
        f = count(lambda kk, ps, cand=cand: jnp.where(kk == tau, ps, 2 ** 30) < cand)
        jj = jnp.where(f <= need, cand, jj)

    def selected(kk, ps):
        return jnp.where(kk > tau, 1.0, jnp.where(jnp.where(kk == tau, ps, 2 ** 30) < jj, 1.0, 0.0))

    msk_ref[...] = selected(key, pos)
    own_ref[...] = jnp.broadcast_to(selected(key_own, pos_own), own_ref.shape)


def _select_sample(page_table, iq8, iw8, ik_own, cache_idx_t, layer, pps):
    Bs, n_pages = page_table.shape
    K = min(TOPK_MAX, (n_pages * PAGE + 1) // 4)
    page_spec = lambda j: pl.BlockSpec((None, None, IDX_DIM, PAGE),
                                       lambda b, p, pt, j=j: (layer, pt[b, p * pps + j], 0, 0))
    grid_spec = pltpu.PrefetchScalarGridSpec(
        num_scalar_prefetch=1,
        grid=(Bs, n_pages // pps),
        in_specs=[pl.BlockSpec((None, IDX_DIM, IDX_HEADS), lambda b, p, pt: (b, 0, 0)),
                  pl.BlockSpec((None, IDX_HEADS, 1), lambda b, p, pt: (b, 0, 0))]
                 + [page_spec(j) for j in range(pps)],
        out_specs=pl.BlockSpec((None, n_pages, PAGE), lambda b, p, pt: (b, 0, 0)),
    )
    scores = pl.pallas_call(
        functools.partial(_score_step_kernel, pps=pps),
        out_shape=jax.ShapeDtypeStruct((Bs, n_pages, PAGE), F32),
        grid_spec=grid_spec,
        compiler_params=_cp("parallel", "arbitrary"),
        name="dsa_score_sample",
    )(page_table, jnp.swapaxes(iq8, 1, 2), iw8, *([cache_idx_t] * pps))
    whole = lambda shape: pl.BlockSpec(shape, lambda i: (0,) * len(shape))
    return pl.pallas_call(
        functools.partial(_threshold_step_kernel, K=K),
        out_shape=[jax.ShapeDtypeStruct((Bs, n_pages, PAGE), F32), jax.ShapeDtypeStruct((Bs, 1, LANES), F32)],
        grid=(1,),
        in_specs=[whole((Bs, n_pages, PAGE)), whole((Bs, IDX_HEADS, IDX_DIM)), whole((Bs, IDX_HEADS, 1)),
                  whole((Bs, 1, IDX_DIM))],
        out_specs=[whole((Bs, n_pages, PAGE)), whole((Bs, 1, LANES))],
        compiler_params=_cp("arbitrary"),
        name="dsa_threshold_sample",
    )(scores, iq8, iw8, ik_own)


def _att_step_kernel(pt_ref, q_ref, msk_ref, own_ref, ko_ref, vo_ref, *refs, pp):
    k_refs, v_refs = refs[:pp], refs[pp:2 * pp]
    o_ref, m_sc, l_sc, acc_sc = refs[2 * pp:]
    p = pl.program_id(1)
    rows = PAGE * KV_HEADS
    grp = ATT_HEADS // KV_HEADS

    @pl.when(p == 0)
    def _():
        m_sc[...] = jnp.full(m_sc.shape, NEG, F32)
        l_sc[...] = jnp.zeros(l_sc.shape, F32)
        acc_sc[...] = jnp.zeros(acc_sc.shape, F32)

    q8 = q_ref[...]
    own_kv = (lax.broadcasted_iota(I32, (ATT_HEADS, rows), 1) % KV_HEADS
              == lax.broadcasted_iota(I32, (ATT_HEADS, rows), 0) // grp)
    expand = jnp.where(lax.broadcasted_iota(I32, (PAGE, rows), 1) // KV_HEADS
                       == lax.broadcasted_iota(I32, (PAGE, rows), 0), 1.0, 0.0).astype(BF16)
    sms = []
    for j in range(pp):
        s = _dot_nt(q8, k_refs[j][...].astype(BF16))
        mrow = jnp.broadcast_to(msk_ref[pl.ds(p * pp + j, 1), :], (ATT_HEADS, PAGE)).astype(BF16)
        picked = _dot(mrow, expand)
        sms.append(jnp.where(jnp.where(own_kv, picked, 0.0) > 0.0, s, NEG))
    m = m_sc[...]
    m_new = m
    for sm in sms:
        m_new = jnp.maximum(m_new, jnp.max(sm, axis=1, keepdims=True))
    alpha = jnp.exp(m - m_new)
    l = l_sc[...] * alpha
    acc = acc_sc[...] * alpha
    for j in range(pp):
        pr = jnp.where(sms[j] > 0.5 * NEG, jnp.exp(sms[j] - m_new), 0.0)
        l = l + jnp.sum(pr, axis=1, keepdims=True)
        acc = acc + _dot(pr.astype(BF16), v_refs[j][...].astype(BF16))
    m_sc[...], l_sc[...], acc_sc[...] = m_new, l, acc

    @pl.when(p == pl.num_programs(1) - 1)
    def _():
        ko = ko_ref[...].astype(BF16).astype(F32)
        s_own = jnp.sum(q8.astype(F32) * ko, axis=1, keepdims=True)
        sel_own = own_ref[:, 0:1] > 0.0
        sm = jnp.where(sel_own, s_own, NEG)
        m_fin = jnp.maximum(m_new, sm)
        a_fin = jnp.exp(m_new - m_fin)
        pr = jnp.where(sel_own, jnp.exp(sm - m_fin), 0.0)
        o = (acc * a_fin + pr * vo_ref[...]) / (l * a_fin + pr)
        o_ref[...] = o.astype(BF16)


def _attend_sample(page_table, q8, msk, own, k_own8, v_own8, cache_k, cache_v, layer, pp):
    Bs, n_pages = page_table.shape
    rows = PAGE * KV_HEADS
    page_spec = lambda j: pl.BlockSpec((None, None, rows, HEAD_DIM),
                                       lambda b, p, pt, j=j: (layer, pt[b, p * pp + j], 0, 0))
    per_b = lambda r, w: pl.BlockSpec((None, r, w), lambda b, p, pt: (b, 0, 0))
    grid_spec = pltpu.PrefetchScalarGridSpec(
        num_scalar_prefetch=1,
        grid=(Bs, n_pages // pp),
        in_specs=[per_b(ATT_HEADS, HEAD_DIM), per_b(n_pages, PAGE), per_b(1, LANES),
                  per_b(ATT_HEADS, HEAD_DIM), per_b(ATT_HEADS, HEAD_DIM)]
                 + [page_spec(j) for j in range(pp)] + [page_spec(j) for j in range(pp)],
        out_specs=per_b(ATT_HEADS, HEAD_DIM),
        scratch_shapes=[pltpu.VMEM((ATT_HEADS, 1), F32), pltpu.VMEM((ATT_HEADS, 1), F32),
                        pltpu.VMEM((ATT_HEADS, HEAD_DIM), F32)],
    )
    return pl.pallas_call(
        functools.partial(_att_step_kernel, pp=pp),
        out_shape=jax.ShapeDtypeStruct((Bs, ATT_HEADS, HEAD_DIM), BF16),
        grid_spec=grid_spec,
        compiler_params=_cp("parallel", "arbitrary"),
        name="dsa_attend_sample",
    )(page_table, q8, msk, own, k_own8, v_own8, *([cache_k] * pp), *([cache_v] * pp))


def _mixout_kernel(x_ref, ro_ref, ca_ref, ao_ref, g0_ref, g1_ref, g2_ref, gm_ref,
                   wr_ref, wc_ref, wa_ref, wo_ref, o_ref):
    merged = (jax.nn.sigmoid(g0_ref[...]) * _dot(ro_ref[...], wr_ref[...])
              + jax.nn.sigmoid(g1_ref[...]) * _dot(ca_ref[...], wc_ref[...])
              + jax.nn.sigmoid(g2_ref[...]) * _dot(ao_ref[...], wa_ref[...]))
    o_ref[...] = x_ref[...] + gm_ref[0] * _dot(merged.astype(BF16), wo_ref[...])


def _mixout(x, ro, ca, ao, z, gate, w_r, w_c, w_a, w_o, tm, rpg):
    M, D = x.shape
    row = pl.BlockSpec((tm, D), lambda i: (i, 0))
    glc = lambda k: pl.BlockSpec((tm, D), lambda i, o=Z_GL // D + k: (i, o))
    wsp = pl.BlockSpec((D, D), lambda i: (0, 0))
    return pl.pallas_call(
        _mixout_kernel,
        out_shape=jax.ShapeDtypeStruct((M, D), F32),
        grid=(M // tm,),
        in_specs=[row, row, row, row, glc(0), glc(1), glc(2), _mod_spec(gate, tm, rpg), wsp, wsp, wsp, wsp],
        out_specs=row,
        compiler_params=_cp("parallel"),
        name="mixer_out",
    )(x, ro, ca, ao, z, z, z, gate, w_r, w_c, w_a, w_o)


def _rope_tables(pos, rot_dim, theta, period):
    half = rot_dim // 2
    inv = 1.0 / (theta ** (jnp.arange(half, dtype=F32) / half))
    ang = pos.astype(F32)[:, None] * inv[None, :]
    lane = np.arange(LANES) % period
    idx = lane % half
    cos = jnp.where(lane < rot_dim, jnp.cos(ang)[:, idx], 1.0)
    sin = jnp.sin(ang)[:, idx]
    s_lo = jnp.where(lane < half, -sin, 0.0)
    s_hi = jnp.where((lane >= half) & (lane < rot_dim), sin, 0.0)
    return cos, s_lo, s_hi


def _tables(pos):
    ca = _rope_tables(pos, ROT_DIM, ROPE_THETA, HEAD_DIM)
    ci = _rope_tables(pos, IDX_ROT, ROPE_THETA, IDX_DIM)
    cr, r_lo, r_hi = _rope_tables(pos, RET_DK, RET_THETA, RET_DK)
    return ca, ci, (cr, r_lo + r_hi)


def _mods(mod, rows):
    parts = jnp.split(mod, 9, axis=-1)
    if rows == 1:
        return [p[:, None, :] for p in parts]
    return [p[None] for p in parts]


def _cat_w_in(w):
    pad = Z_END - Z_IX - (IDX_DIM + IDX_HEADS)
    return jnp.concatenate([w[:, :6144], w[:, 7752:10824], w[:, 6144:7680], w[:, 7680:7752],
                            jnp.zeros((w.shape[0], pad), w.dtype)], axis=1).astype(BF16)


def kernel(x_prompt, x_sample, cache_k, cache_v, cache_idx_k, state_ret, state_conv, page_table, c_prompt, c_sample, w_ada, b_ada, norm_g, w_ff1_up, w_ff1_down, w_ff2_up, w_ff2_down, w_in, conv_w, conv_b, conv_g, q_norm_g, k_norm_g, idx_k_norm_g, w_ret_o, w_conv_o, w_att_o, w_o):
    B, T, D = x_prompt.shape
    Bs, Ts, _ = x_sample.shape
    assert Ts == 1 and T % 512 == 0 and D == 1024
    depth = w_ada.shape[0]
    n_pages = page_table.shape[1]
    past = n_pages * PAGE
    tm_p = 512
    ck = 256
    pps = 16 if n_pages % 16 == 0 else 4
    pp = 32 if n_pages % 32 == 0 else 4
    grp = ATT_HEADS // KV_HEADS

    tabs_p = _tables(jnp.arange(T, dtype=jnp.int32))
    tabs_s = _tables(jnp.full((Bs,), past, dtype=jnp.int32))
    lg = jnp.broadcast_to(jnp.log(1.0 - 2.0 ** (-5.0 - jnp.arange(RET_HEADS, dtype=F32)))[:, None],
                          (RET_HEADS, LANES))
    c_all = jnp.concatenate([c_prompt, c_sample], axis=0)
    ck4 = cache_k.reshape(cache_k.shape[0], cache_k.shape[1], PAGE * KV_HEADS, HEAD_DIM)
    cv4 = cache_v.reshape(cache_v.shape[0], cache_v.shape[1], PAGE * KV_HEADS, HEAD_DIM)
    cache_idx_t = jnp.swapaxes(cache_idx_k, 2, 3)
    conv_b3 = conv_b.reshape(depth, 1, D)
    conv_g3 = conv_g.reshape(depth, 1, D)

    xp = x_prompt.reshape(B * T, D)
    xs = x_sample.reshape(Bs, D)
    outs_p, outs_s = [], []
    for l in range(depth):
        mod = _ada_mod(c_all, w_ada, b_ada, l)
        mp = _mods(mod[:B], 1)
        msm = _mods(mod[B:], Bs)
        wu1, wd1 = w_ff1_up[l].astype(BF16), w_ff1_down[l].astype(BF16)
        wu2, wd2 = w_ff2_up[l].astype(BF16), w_ff2_down[l].astype(BF16)
        w_cat = _cat_w_in(w_in[l])
        w_r, w_c = w_ret_o[l].astype(BF16), w_conv_o[l].astype(BF16)
        w_a, w_oo = w_att_o[l].astype(BF16), w_o[l].astype(BF16)
        ng = norm_g[l]
        qg, kg = q_norm_g[l][None, :], k_norm_g[l][None, :]
        ikg = jnp.concatenate([idx_k_norm_g[l], jnp.zeros((LANES - IDX_DIM,), F32)])[None, :]

        xp = _ffn(xp, mp[0], mp[1], mp[2], ng[0:1], wu1, wd1, tm_p, T)
        z = _inproj(xp, mp[3], mp[4], ng[1:2], w_cat, min(1024, T), T)
        (qa, k_p, kb, v_p, _iqr, _ixo, idxk_p, vt3, iqth, iqtl, ik3, iwt) = _prep(
            z, tabs_p[0], tabs_p[1], qg, kg, ikg, tm_p, True, ck)
        ro, ret_p = _retention_prompt(z.reshape(B, T, Z_END), tabs_p[2][0], tabs_p[2][1], lg)
        ca, conv_p = _conv_prompt(z, B, T, conv_w, conv_b3, conv_g3, l, tm_p)
        mask = _select_prompt(ik3, iqth, iqtl, iwt, B, T, 2 * ck)
        ao = _attend_prompt(qa, kb, vt3, mask, B, T, ck)
        xp = _mixout(xp, ro.reshape(B * T, D), ca, ao, z, mp[5], w_r, w_c, w_a, w_oo, tm_p, T)
        xp = _ffn(xp, mp[6], mp[7], mp[8], ng[2:3], wu2, wd2, tm_p, T)
        outs_p.append((k_p.reshape(B, T, KV_HEADS, HEAD_DIM), v_p.reshape(B, T, KV_HEADS, HEAD_DIM),
                       idxk_p.reshape(B, T, IDX_DIM), ret_p, conv_p))

        xs = _ffn(xs, msm[0], msm[1], msm[2], ng[0:1], wu1, wd1, Bs, Bs)
        zs = _inproj(xs, msm[3], msm[4], ng[1:2], w_cat, Bs, Bs)
        qa_s, k_s, _kb, v_s, iqr_s, ixo_s, idxk_s = _prep(
            zs, tabs_s[0], tabs_s[1], qg, kg, ikg, Bs, False, ck)
        zs3 = zs.reshape(Bs, 1, Z_END)
        ro_s, ret_s = _retention_sample(zs3, state_ret, l, tabs_s[2][0][0:1], tabs_s[2][1][0:1], lg)
        ca_s, conv_s = _conv_sample(zs3, state_conv, conv_w, conv_b3, conv_g3, l)
        iq8 = iqr_s.reshape(Bs, IDX_HEADS, IDX_DIM)
        iw8 = ixo_s[:, IDX_DIM:IDX_DIM + IDX_HEADS].reshape(Bs, IDX_HEADS, 1)
        msk, own = _select_sample(page_table, iq8, iw8, idxk_s.reshape(Bs, 1, IDX_DIM), cache_idx_t, l, pps)
        k_own8 = jnp.repeat(k_s.reshape(Bs, KV_HEADS, HEAD_DIM), grp, axis=1)
        v_own8 = jnp.repeat(v_s.reshape(Bs, KV_HEADS, HEAD_DIM), grp, axis=1)
        ao_s = _attend_sample(page_table, qa_s.reshape(Bs, ATT_HEADS, HEAD_DIM), msk, own,
                              k_own8, v_own8, ck4, cv4, l, pp)
        xs = _mixout(xs, ro_s.reshape(Bs, D), ca_s.reshape(Bs, D), ao_s.reshape(Bs, D), zs, msm[5],
                     w_r, w_c, w_a, w_oo, Bs, Bs)
        xs = _ffn(xs, msm[6], msm[7], msm[8], ng[2:3], wu2, wd2, Bs, Bs)
        outs_s.append((k_s.reshape(Bs, 1, KV_HEADS, HEAD_DIM), v_s.reshape(Bs, 1, KV_HEADS, HEAD_DIM),
                       idxk_s.reshape(Bs, 1, IDX_DIM), ret_s, conv_s))

    stack = lambda outs, j: jnp.stack([o[j] for o in outs])
    return (xp.reshape(B, T, D), xs.reshape(Bs, 1, D),
            stack(outs_p, 0), stack(outs_p, 1), stack(outs_p, 2), stack(outs_p, 3), stack(outs_p, 4),
            stack(outs_s, 0), stack(outs_s, 1), stack(outs_s, 2), stack(outs_s, 3), stack(outs_s, 4))
```

```python
import functools

import numpy as np
import jax
import jax.numpy as jnp
from jax import lax
from jax.experimental import pallas as pl
from jax.experimental.pallas import tpu as pltpu

F32 = jnp.float32
BF16 = jnp.bfloat16
I32 = jnp.int32

EPS = 1e-6
LANES = 128
SUBLANES = 8
NEG = -1e30
MASK_BIAS = -1e30
M_INIT = -1e20
INT_MIN = -2 ** 31
COUNT_ROWS = 64
RET_HEADS = 4
RET_DK = 128
RET_DV = 256
RET_CHUNK = 128
RET_THETA = 10000.0
CONV_W = 31
ATT_HEADS = 8
KV_HEADS = 4
HEAD_DIM = 128
IDX_HEADS = 8
IDX_DIM = 64
TOPK_MAX = 256
ROPE_THETA = 500000.0
ROT_DIM = HEAD_DIM // 4
IDX_ROT = IDX_DIM // 4
PAGE = 128

Z_RQ, Z_RK, Z_RV, Z_RG = 0, 512, 1024, 2048
Z_CU, Z_CG, Z_AQ, Z_GL = 3072, 4096, 5120, 6144
Z_AK, Z_AV, Z_IQ, Z_IX, Z_END = 9216, 9728, 10240, 10752, 10880

VMEM_LIMIT = 56 * 1024 * 1024


def _cp(*sem):
    return pltpu.CompilerParams(dimension_semantics=tuple(sem), vmem_limit_bytes=VMEM_LIMIT)


def _dot(a, b):
    return jnp.dot(a, b, preferred_element_type=F32)


def _dot_nt(a, b):
    return lax.dot_general(a, b, (((1,), (1,)), ((), ())), preferred_element_type=F32)


def _split(x):
    hi = x.astype(BF16)
    lo = (x - hi.astype(F32)).astype(BF16)
    return hi, lo


def _dot3(a, b, nt=False):
    d = _dot_nt if nt else _dot
    ah, al = _split(a)
    bh, bl = _split(b)
    return d(ah, bh) + d(ah, bl) + d(al, bh)


def _silu(x):
    return x * jax.nn.sigmoid(x)


def _rms(x):
    return x * lax.rsqrt(jnp.mean(x * x, axis=-1, keepdims=True) + EPS)


def _norm_mod(x, g, shift, scale):
    return _rms(x) * g * (1.0 + scale) + shift


def _rot(x, cos, s_lo, s_hi, half):
    return x * cos + pltpu.roll(x, LANES - half, 1) * s_lo + pltpu.roll(x, half, 1) * s_hi


def _sortable(s):
    s = jnp.where(s == 0.0, 0.0, s)
    bits = pltpu.bitcast(s, I32)
    return bits ^ ((bits >> 31) & 0x7FFFFFFF)


def _ada_kernel(c_ref, w_ref, b_ref, o_ref):
    o_ref[...] = _dot3(_silu(c_ref[...]), w_ref[...]) + b_ref[...]


def _ada_mod(c, w_ada, b_ada, layer):
    R, D = c.shape
    N = w_ada.shape[2]
    tn = 1024
    return pl.pallas_call(
        _ada_kernel,
        out_shape=jax.ShapeDtypeStruct((R, N), F32),
        grid=(N // tn,),
        in_specs=[pl.BlockSpec((R, D), lambda j: (0, 0)),
                  pl.BlockSpec((None, D, tn), lambda j: (layer, 0, j)),
                  pl.BlockSpec((None, 1, tn), lambda j: (layer, 0, j))],
        out_specs=pl.BlockSpec((R, tn), lambda j: (0, j)),
        compiler_params=_cp("parallel"),
        name="ada_mod",
    )(c, w_ada, b_ada.reshape(b_ada.shape[0], 1, N))


def _ffn_kernel(x_ref, sh_ref, sc_ref, g_ref, ng_ref, wu_ref, wd_ref, o_ref, acc_ref, *, ff, tf):
    x = x_ref[...]
    h = _norm_mod(x, ng_ref[...], sh_ref[0], sc_ref[0]).astype(BF16)
    for j in range(ff // tf):
        a = _dot(h, wu_ref[:, j * tf:(j + 1) * tf])
        b = _dot(h, wu_ref[:, ff + j * tf:ff + (j + 1) * tf])
        act = (_silu(a) * b).astype(BF16)
        upd = _dot(act, wd_ref[j * tf:(j + 1) * tf, :])
        if j == 0:
            acc_ref[...] = upd
        else:
            acc_ref[...] += upd
    o_ref[...] = x + 0.5 * g_ref[0] * acc_ref[...]


def _mod_spec(mod, tm, rpg):
    rm = mod.shape[1]
    return pl.BlockSpec((1, rm, mod.shape[2]), lambda i: ((i * tm) // rpg, 0, 0))


def _ffn(x, shift, scale, gate, ng, w_up, w_down, tm, rpg):
    M, D = x.shape
    ff = w_down.shape[0]
    const = lambda i: (0, 0)
    return pl.pallas_call(
        functools.partial(_ffn_kernel, ff=ff, tf=256),
        out_shape=jax.ShapeDtypeStruct((M, D), F32),
        grid=(M // tm,),
        in_specs=[pl.BlockSpec((tm, D), lambda i: (i, 0)),
                  _mod_spec(shift, tm, rpg), _mod_spec(scale, tm, rpg), _mod_spec(gate, tm, rpg),
                  pl.BlockSpec((1, D), const),
                  pl.BlockSpec((D, 2 * ff), const),
                  pl.BlockSpec((ff, D), const)],
        out_specs=pl.BlockSpec((tm, D), lambda i: (i, 0)),
        scratch_shapes=[pltpu.VMEM((tm, D), F32)],
        compiler_params=_cp("parallel"),
        name="ffn",
    )(x, shift, scale, gate, ng, w_up, w_down)


def _inproj_kernel(x_ref, sh_ref, sc_ref, ng_ref, w_ref, o_ref, h_ref):
    @pl.when(pl.program_id(1) == 0)
    def _():
        h_ref[...] = _norm_mod(x_ref[...], ng_ref[...], sh_ref[0], sc_ref[0]).astype(BF16)

    o_ref[...] = _dot(h_ref[...], w_ref[...])


def _inproj(x, shift, scale, ng, w_cat, tm, rpg):
    M, D = x.shape
    N = w_cat.shape[1]
    tn = N // 5
    ms = lambda mod: pl.BlockSpec((1, mod.shape[1], D), lambda i, j: ((i * tm) // rpg, 0, 0))
    return pl.pallas_call(
        _inproj_kernel,
        out_shape=jax.ShapeDtypeStruct((M, N), F32),
        grid=(M // tm, N // tn),
        in_specs=[pl.BlockSpec((tm, D), lambda i, j: (i, 0)),
                  ms(shift), ms(scale),
                  pl.BlockSpec((1, D), lambda i, j: (0, 0)),
                  pl.BlockSpec((D, tn), lambda i, j: (0, j))],
        out_specs=pl.BlockSpec((tm, tn), lambda i, j: (i, j)),
        scratch_shapes=[pltpu.VMEM((tm, D), BF16)],
        compiler_params=_cp("parallel", "arbitrary"),
        name="inproj",
    )(x, shift, scale, ng, w_cat)


def _prep_kernel(aq_ref, ak_ref, av_ref, iq_ref, ix_ref,
                 ca_ref, sa1_ref, sa2_ref, ci_ref, si1_ref, si2_ref, qg_ref, kg_ref, ikg_ref,
                 qa_ref, k_ref, kb_ref, v_ref, iqr_ref, ixo_ref, idxk_ref, *t_refs, tck):
    ca, sa1, sa2 = ca_ref[...], sa1_ref[...], sa2_ref[...]
    ci, si1, si2 = ci_ref[...], si1_ref[...], si2_ref[...]
    qg, kg = qg_ref[...], kg_ref[...]
    q_scale = HEAD_DIM ** -0.5
    for h in range(ATT_HEADS):
        sl = slice(h * HEAD_DIM, (h + 1) * HEAD_DIM)
        y = _rot(_rms(aq_ref[:, sl]) * qg, ca, sa1, sa2, ROT_DIM // 2)
        qa_ref[:, sl] = (y * q_scale).astype(BF16)
    for h in range(KV_HEADS):
        sl = slice(h * HEAD_DIM, (h + 1) * HEAD_DIM)
        y = _rot(_rms(ak_ref[:, sl]) * kg, ca, sa1, sa2, ROT_DIM // 2)
        k_ref[:, sl] = y
        kb_ref[:, sl] = y.astype(BF16)
    av = av_ref[...]
    v_ref[...] = av
    for j in range(IDX_HEADS * IDX_DIM // LANES):
        sl = slice(j * LANES, (j + 1) * LANES)
        iqr_ref[:, sl] = _rot(iq_ref[:, sl], ci, si1, si2, IDX_ROT // 2)
    blk = ix_ref[...]
    lane = lax.broadcasted_iota(I32, blk.shape, 1)
    ikm = jnp.where(lane < IDX_DIM, blk, 0.0)
    ms = jnp.sum(ikm * ikm, axis=-1, keepdims=True) * (1.0 / IDX_DIM)
    ikr = _rot(ikm * lax.rsqrt(ms + EPS) * ikg_ref[...], ci, si1, si2, IDX_ROT // 2)
    iw_scale = IDX_HEADS ** -0.5 * IDX_DIM ** -0.5
    iws = jnp.where((lane >= IDX_DIM) & (lane < IDX_DIM + IDX_HEADS), blk * iw_scale, 0.0)
    ixo = ikr + iws
    ixo_ref[...] = ixo
    idxk_ref[...] = ikr[:, :IDX_DIM]
    if t_refs:
        vt_ref, iqth_ref, iqtl_ref, ik3_ref, iwt_ref = t_refs
        tm = av.shape[0]
        avt = av.T
        for c in range(tm // tck):
            vt_ref[c] = avt[:, c * tck:(c + 1) * tck].astype(BF16)
        iqt = iqr_ref[...].T
        hi = iqt.astype(BF16)
        iqth_ref[...] = hi
        iqtl_ref[...] = (iqt - hi.astype(F32)).astype(BF16)
        khi = ikr.astype(BF16).astype(F32)
        klo = ikr - khi
        ik3_ref[...] = jnp.concatenate([khi + pltpu.roll(klo, IDX_DIM, 1), khi], axis=1).astype(BF16)
        iwt_ref[...] = ixo.T[IDX_DIM:IDX_DIM + IDX_HEADS, :]


def _prep(z, tabs_a, tabs_i, qg, kg, ikg, tm, transposed, tck):
    M = z.shape[0]
    nt = M // tm
    tpb = tabs_a[0].shape[0] // tm
    tab = pl.BlockSpec((tm, LANES), lambda i: (i % tpb, 0))
    col = lambda w, off: pl.BlockSpec((tm, w), lambda i, o=off // w: (i, o))
    row = lambda w: pl.BlockSpec((tm, w), lambda i: (i, 0))
    vec = pl.BlockSpec((1, LANES), lambda i: (0, 0))
    in_specs = [col(1024, Z_AQ), col(512, Z_AK), col(512, Z_AV), col(512, Z_IQ), col(128, Z_IX)]
    in_specs += [tab] * 6 + [vec] * 3
    out_shape = [jax.ShapeDtypeStruct((M, 1024), BF16), jax.ShapeDtypeStruct((M, 512), F32),
                 jax.ShapeDtypeStruct((M, 512), BF16), jax.ShapeDtypeStruct((M, 512), F32),
                 jax.ShapeDtypeStruct((M, 512), F32), jax.ShapeDtypeStruct((M, LANES), F32),
                 jax.ShapeDtypeStruct((M, IDX_DIM), F32)]
    out_specs = [row(1024), row(512), row(512), row(512), row(512), row(LANES), row(IDX_DIM)]
    if transposed:
        out_shape += [jax.ShapeDtypeStruct((M // tck, 512, tck), BF16),
                      jax.ShapeDtypeStruct((512, M), BF16), jax.ShapeDtypeStruct((512, M), BF16),
                      jax.ShapeDtypeStruct((M, 256), BF16), jax.ShapeDtypeStruct((IDX_HEADS, M), F32)]
        out_specs += [pl.BlockSpec((tm // tck, 512, tck), lambda i: (i, 0, 0)),
                      pl.BlockSpec((512, tm), lambda i: (0, i)), pl.BlockSpec((512, tm), lambda i: (0, i)),
                      row(256), pl.BlockSpec((IDX_HEADS, tm), lambda i: (0, i))]
    return pl.pallas_call(
        functools.partial(_prep_kernel, tck=tck),
        out_shape=out_shape, grid=(nt,), in_specs=in_specs, out_specs=out_specs,
        compiler_params=_cp("parallel"), name="dsa_prep",
    )(z, z, z, z, z, *tabs_a, *tabs_i, qg, kg, ikg)


def _ret_kernel(q_ref, k_ref, v_ref, rg_ref, cos_ref, sin_ref, lg_ref, o_ref, st_ref, s_ref, *, nb):
    n = pl.program_id(0)
    C = RET_CHUNK

    @pl.when(n == 0)
    def _():
        s_ref[...] = jnp.zeros_like(s_ref)

    cos, sin = cos_ref[...], sin_ref[...]
    ri = lax.broadcasted_iota(I32, (C, C), 0).astype(F32)
    ci = lax.broadcasted_iota(I32, (C, C), 1).astype(F32)
    diff = ri - ci
    rv = lax.broadcasted_iota(I32, (C, RET_DV), 0).astype(F32)
    for h in range(RET_HEADS):
        lg = lg_ref[h:h + 1, :]
        lg2 = jnp.concatenate([lg, lg], axis=1)
        dmask = jnp.where(diff >= 0, jnp.exp(jnp.maximum(diff, 0.0) * lg), 0.0)
        cross = jnp.exp((rv + 1.0) * lg2)
        kdec = jnp.exp((C - 1.0 - ri) * lg)
        g_c = jnp.exp(C * lg2)
        for b in range(nb):
            qs = slice(h * RET_DK, (h + 1) * RET_DK)
            vs = slice(h * RET_DV, (h + 1) * RET_DV)
            q = q_ref[b, :, qs]
            k = k_ref[b, :, qs]
            v = v_ref[b, :, vs]
            qr = q * cos + pltpu.roll(q, RET_DK // 2, 1) * sin
            kr = (k * cos + pltpu.roll(k, RET_DK // 2, 1) * sin) * (RET_DK ** -0.5)
            s_old = s_ref[b * RET_HEADS + h]
            sc = _dot3(qr, kr, nt=True) * dmask
            out = _dot3(sc, v) + _dot3(qr, s_old) * cross
            s_ref[b * RET_HEADS + h] = g_c * s_old + _dot3((kr * kdec).T, v)
            rg = rg_ref[b, :, vs]
            o_ref[b, :, vs] = (_silu(rg) * _rms(out)).astype(BF16)

    @pl.when(n == pl.num_programs(0) - 1)
    def _():
        for b in range(nb):
            for h in range(RET_HEADS):
                st_ref[b, h] = s_ref[b * RET_HEADS + h]


def _retention_prompt(z3, cos_r, sin_r, lg):
    B, T, _ = z3.shape
    C = RET_CHUNK
    return pl.pallas_call(
        functools.partial(_ret_kernel, nb=B),
        out_shape=[jax.ShapeDtypeStruct((B, T, 1024), BF16),
                   jax.ShapeDtypeStruct((B, RET_HEADS, RET_DK, RET_DV), F32)],
        grid=(T // C,),
        in_specs=[pl.BlockSpec((B, C, 512), lambda n: (0, n, Z_RQ // 512)),
                  pl.BlockSpec((B, C, 512), lambda n: (0, n, Z_RK // 512)),
                  pl.BlockSpec((B, C, 1024), lambda n: (0, n, Z_RV // 1024)),
                  pl.BlockSpec((B, C, 1024), lambda n: (0, n, Z_RG // 1024)),
                  pl.BlockSpec((C, LANES), lambda n: (n, 0)),
                  pl.BlockSpec((C, LANES), lambda n: (n, 0)),
                  pl.BlockSpec((RET_HEADS, LANES), lambda n: (0, 0))],
        out_specs=[pl.BlockSpec((B, C, 1024), lambda n: (0, n, 0)),
                   pl.BlockSpec((B, RET_HEADS, RET_DK, RET_DV), lambda n: (0, 0, 0, 0))],
        scratch_shapes=[pltpu.VMEM((B * RET_HEADS, RET_DK, RET_DV), F32)],
        compiler_params=_cp("arbitrary"),
        name="retention_prompt",
    )(z3, z3, z3, z3, cos_r, sin_r, lg)


def _ret_step_kernel(q_ref, k_ref, v_ref, rg_ref, s0_ref, cos_ref, sin_ref, lg_ref, o_ref, st_ref):
    cos, sin = cos_ref[...], sin_ref[...]
    eye = (lax.broadcasted_iota(I32, (RET_DK, RET_DK), 0) == lax.broadcasted_iota(I32, (RET_DK, RET_DK), 1))
    for h in range(RET_HEADS):
        qs = slice(h * RET_DK, (h + 1) * RET_DK)
        vs = slice(h * RET_DV, (h + 1) * RET_DV)
        q, k, v = q_ref[:, qs], k_ref[:, qs], v_ref[:, vs]
        qr = q * cos + pltpu.roll(q, RET_DK // 2, 1) * sin
        kr = (k * cos + pltpu.roll(k, RET_DK // 2, 1) * sin) * (RET_DK ** -0.5)
        lg = lg_ref[h:h + 1, :]
        gamma = jnp.exp(jnp.concatenate([lg, lg], axis=1))
        qcol = jnp.sum(jnp.where(eye, qr, 0.0), axis=1, keepdims=True)
        kcol = jnp.sum(jnp.where(eye, kr, 0.0), axis=1, keepdims=True)
        s0 = s0_ref[h]
        qk = jnp.sum(qr * kr, axis=1, keepdims=True)
        out = qk * v + jnp.sum(qcol * s0, axis=0, keepdims=True) * gamma
        st_ref[h] = gamma * s0 + kcol * v
        o_ref[:, vs] = (_silu(rg_ref[:, vs]) * _rms(out)).astype(BF16)


def _retention_sample(z3, state_ret, layer, cos_r, sin_r, lg):
    Bs = z3.shape[0]
    zc = lambda w, off: pl.BlockSpec((None, 1, w), lambda b, o=off // w: (b, 0, o))
    return pl.pallas_call(
        _ret_step_kernel,
        out_shape=[jax.ShapeDtypeStruct((Bs, 1, 1024), BF16),
                   jax.ShapeDtypeStruct((Bs, RET_HEADS, RET_DK, RET_DV), F32)],
        grid=(Bs,),
        in_specs=[zc(512, Z_RQ), zc(512, Z_RK), zc(1024, Z_RV), zc(1024, Z_RG),
                  pl.BlockSpec((None, None, RET_HEADS, RET_DK, RET_DV), lambda b: (layer, b, 0, 0, 0)),
                  pl.BlockSpec((1, LANES), lambda b: (0, 0)),
                  pl.BlockSpec((1, LANES), lambda b: (0, 0)),
                  pl.BlockSpec((RET_HEADS, LANES), lambda b: (0, 0))],
        out_specs=[pl.BlockSpec((None, 1, 1024), lambda b: (b, 0, 0)),
                   pl.BlockSpec((None, RET_HEADS, RET_DK, RET_DV), lambda b: (b, 0, 0, 0))],
        compiler_params=_cp("parallel"),
        name="retention_sample",
    )(z3, z3, z3, z3, state_ret, cos_r, sin_r, lg)


CONV_HALO = 32
CONV_RB = 32
CONV_CB = 512


def _conv_kernel(a_ref, g_ref, w_ref, b_ref, cg_ref, o_ref, cs_ref, u_ref, sh_ref, y_ref, wb_ref):
    t = pl.program_id(1)
    tm = a_ref.shape[0]

    @pl.when(t == 0)
    def _():
        u_ref[0:CONV_HALO, :] = jnp.zeros((CONV_HALO, u_ref.shape[1]), F32)
        for j in range(CONV_W):
            wb_ref[j] = jnp.broadcast_to(w_ref[j:j + 1, :], (SUBLANES, w_ref.shape[1]))

    u_ref[CONV_HALO:CONV_HALO + tm, :] = a_ref[...] * jax.nn.sigmoid(g_ref[...])
    span = tm + CONV_HALO - SUBLANES
    for s in range(1, SUBLANES):
        sh_ref[s - 1, 0:span, :] = u_ref[s:s + span, :]
    first = CONV_HALO - (CONV_W - 1)
    for r in range(tm // CONV_RB):
        for c in range(u_ref.shape[1] // CONV_CB):
            cs = slice(c * CONV_CB, (c + 1) * CONV_CB)
            acc = jnp.zeros((CONV_RB // SUBLANES, SUBLANES, CONV_CB), F32)
            for j in range(CONV_W):
                s = (first + j) % SUBLANES
                lo = r * CONV_RB + first + j - s
                src = u_ref if s == 0 else sh_ref.at[s - 1]
                tap = src[lo:lo + CONV_RB, cs].reshape(CONV_RB // SUBLANES, SUBLANES, CONV_CB)
                acc = acc + wb_ref[j, :, cs][None] * tap
            y_ref[r * CONV_RB:(r + 1) * CONV_RB, cs] = acc.reshape(CONV_RB, CONV_CB)
    cy = y_ref[...] + b_ref[...]
    o_ref[...] = _silu(_rms(cy) * cg_ref[...]).astype(BF16)

    @pl.when(t == pl.num_programs(1) - 1)
    def _():
        cs_ref[...] = u_ref[tm + first:tm + CONV_HALO, :]

    u_ref[0:CONV_HALO, :] = u_ref[tm:tm + CONV_HALO, :]


def _conv_prompt(z, B, T, conv_w, conv_b, conv_g, layer, tm):
    M, C = B * T, 1024
    nt = T // tm
    return pl.pallas_call(
        _conv_kernel,
        out_shape=[jax.ShapeDtypeStruct((M, C), BF16), jax.ShapeDtypeStruct((B, CONV_W - 1, C), F32)],
        grid=(B, nt),
        in_specs=[pl.BlockSpec((tm, C), lambda b, t: (b * nt + t, Z_CU // C)),
                  pl.BlockSpec((tm, C), lambda b, t: (b * nt + t, Z_CG // C)),
                  pl.BlockSpec((None, CONV_W, C), lambda b, t: (layer, 0, 0)),
                  pl.BlockSpec((None, 1, C), lambda b, t: (layer, 0, 0)),
                  pl.BlockSpec((None, 1, C), lambda b, t: (layer, 0, 0))],
        out_specs=[pl.BlockSpec((tm, C), lambda b, t: (b * nt + t, 0)),
                   pl.BlockSpec((None, CONV_W - 1, C), lambda b, t: (b, 0, 0))],
        scratch_shapes=[pltpu.VMEM((tm + CONV_HALO, C), F32), pltpu.VMEM((SUBLANES - 1, tm + CONV_HALO, C), F32),
                        pltpu.VMEM((tm, C), F32), pltpu.VMEM((CONV_W, SUBLANES, C), F32)],
        compiler_params=_cp("parallel", "arbitrary"),
        name="conv_prompt",
    )(z, z, conv_w, conv_b, conv_g)


def _conv_step_kernel(a_ref, g_ref, buf_ref, w_ref, b_ref, cg_ref, o_ref, cs_ref):
    u = a_ref[...] * jax.nn.sigmoid(g_ref[...])
    buf = buf_ref[...]
    y = jnp.sum(w_ref[0:CONV_W - 1, :] * buf, axis=0, keepdims=True) + w_ref[CONV_W - 1:CONV_W, :] * u
    cy = y + b_ref[...]
    o_ref[...] = _silu(_rms(cy) * cg_ref[...]).astype(BF16)
    cs_ref[0:CONV_W - 2, :] = buf[1:CONV_W - 1, :]
    cs_ref[CONV_W - 2:CONV_W - 1, :] = u


def _conv_sample(z3, state_conv, conv_w, conv_b, conv_g, layer):
    Bs, C = z3.shape[0], 1024
    return pl.pallas_call(
        _conv_step_kernel,
        out_shape=[jax.ShapeDtypeStruct((Bs, 1, C), BF16), jax.ShapeDtypeStruct((Bs, CONV_W - 1, C), F32)],
        grid=(Bs,),
        in_specs=[pl.BlockSpec((None, 1, C), lambda b: (b, 0, Z_CU // C)),
                  pl.BlockSpec((None, 1, C), lambda b: (b, 0, Z_CG // C)),
                  pl.BlockSpec((None, None, CONV_W - 1, C), lambda b: (layer, b, 0, 0)),
                  pl.BlockSpec((None, CONV_W, C), lambda b: (layer, 0, 0)),
                  pl.BlockSpec((None, 1, C), lambda b: (layer, 0, 0)),
                  pl.BlockSpec((None, 1, C), lambda b: (layer, 0, 0))],
        out_specs=[pl.BlockSpec((None, 1, C), lambda b: (b, 0, 0)),
                   pl.BlockSpec((None, CONV_W - 1, C), lambda b: (b, 0, 0))],
        compiler_params=_cp("parallel"),
        name="conv_sample",
    )(z3, z3, state_conv, conv_w, conv_b, conv_g)


def _sel_kernel(ik3_ref, iqh_ref, iql_ref, iw_ref, m_ref, key_ref, w_ref, j_ref, *, T, K, ck):
    i = pl.program_id(1)
    nq = LANES
    for h in range(IDX_HEADS):
        rs = slice(h * IDX_DIM, (h + 1) * IDX_DIM)
        cs = slice(h * nq, (h + 1) * nq)
        hi = iqh_ref[rs, :]
        w_ref[0:IDX_DIM, cs] = hi
        w_ref[IDX_DIM:2 * IDX_DIM, cs] = hi
        w_ref[2 * IDX_DIM:3 * IDX_DIM, cs] = iql_ref[rs, :]
        w_ref[3 * IDX_DIM:4 * IDX_DIM, cs] = jnp.zeros((IDX_DIM, nq), BF16)
    nch = ((i + 1) * nq + ck - 1) // ck
    tq = i * nq + lax.broadcasted_iota(I32, (ck, nq), 1)
    row = lax.broadcasted_iota(I32, (ck, nq), 0)
    iw = iw_ref[...]

    def score_body(c, carry):
        off = pl.multiple_of(c * ck, ck)
        d = _dot(ik3_ref[pl.ds(off, ck), :], w_ref[...])
        s = jnp.zeros((ck, nq), F32)
        for h in range(IDX_HEADS):
            s = s + jnp.maximum(d[:, h * nq:(h + 1) * nq], 0.0) * iw[h:h + 1, :]
        key_ref[pl.ds(off, ck), :] = jnp.where(off + row <= tq, _sortable(s), INT_MIN)
        return carry

    lax.fori_loop(0, nch, score_body, 0)

    def count(pred):
        def body(c, acc):
            off = pl.multiple_of(c * ck, ck)
            hit = jnp.where(pred(key_ref[pl.ds(off, ck), :], off), 1.0, 0.0)
            return acc + jnp.sum(hit.reshape(ck // COUNT_ROWS, COUNT_ROWS, nq), axis=0)
        acc = lax.fori_loop(0, nch, body, jnp.zeros((COUNT_ROWS, nq), F32))
        return jnp.sum(acc, axis=0, keepdims=True)

    tau = jnp.full((1, nq), INT_MIN, I32)
    n_ge = (i * nq + lax.broadcasted_iota(I32, (1, nq), 1) + 1).astype(F32)
    for bit in range(31, -1, -1):
        cand = jnp.zeros((1, nq), I32) if bit == 31 else tau | (1 << bit)
        cnt = count(lambda kk, off, cand=cand: kk >= cand)
        ok = cnt >= K
        tau = jnp.where(ok, cand, tau)
        n_ge = jnp.where(ok, cnt, n_ge)
    tau = jnp.maximum(tau, INT_MIN + 1)
    j_ref[...] = jnp.full((1, nq), 2 ** 30, I32)

    @pl.when(jnp.max(n_ge) > K)
    def _():
        need = K - count(lambda kk, off: kk > tau)
        jj = jnp.zeros((1, nq), I32)
        for bit in range(T.bit_length() - 1, -1, -1):
            cand = jj | (1 << bit)
            f = count(lambda kk, off, cand=cand: jnp.where(kk == tau, off + row, 2 ** 30) < cand)
            jj = jnp.where(f <= need, cand, jj)
        j_ref[...] = jj

    jlim = j_ref[...]

    def write_body(c, carry):
        off = pl.multiple_of(c * ck, ck)
        kk = key_ref[pl.ds(off, ck), :]
        tie_pos = jnp.where(kk == tau, off + row, 2 ** 30)
        bias = jnp.where(kk > tau, 0.0, jnp.where(tie_pos < jlim, 0.0, MASK_BIAS))
        m_ref[pl.ds(off, ck), :] = bias.astype(BF16)
        return carry

    lax.fori_loop(0, nch, write_body, 0)

    def fill_body(c, carry):
        off = pl.multiple_of(c * ck, ck)
        m_ref[pl.ds(off, ck), :] = jnp.full((ck, nq), MASK_BIAS, BF16)
        return carry

    lax.fori_loop(nch, T // ck, fill_body, 0)


def _select_prompt(ik3, iqth, iqtl, iwt, B, T, ck):
    nq = T // LANES
    K = min(TOPK_MAX, T // 4)
    return pl.pallas_call(
        functools.partial(_sel_kernel, T=T, K=K, ck=ck),
        out_shape=jax.ShapeDtypeStruct((B, T, T), BF16),
        grid=(B, nq),
        in_specs=[pl.BlockSpec((T, 256), lambda b, i: (b, 0)),
                  pl.BlockSpec((512, LANES), lambda b, i: (0, b * nq + i)),
                  pl.BlockSpec((512, LANES), lambda b, i: (0, b * nq + i)),
                  pl.BlockSpec((IDX_HEADS, LANES), lambda b, i: (0, b * nq + i))],
        out_specs=pl.BlockSpec((None, T, LANES), lambda b, i: (b, 0, i)),
        scratch_shapes=[pltpu.VMEM((T, LANES), I32), pltpu.VMEM((256, IDX_HEADS * LANES), BF16),
                        pltpu.VMEM((1, LANES), I32)],
        compiler_params=_cp("parallel", "arbitrary"),
        name="dsa_select_prompt",
    )(ik3, iqth, iqtl, iwt)


def _att_kernel(q_ref, k_ref, vt_ref, m_ref, o_ref, p_ref, s_ref, *, ck, big):
    i = pl.program_id(2)
    nq = LANES
    n_it = jnp.maximum(((i + 1) * nq + big - 1) // big, 2)
    q2 = q_ref[...]
    qs = jnp.concatenate([q2[:, :HEAD_DIM], q2[:, HEAD_DIM:]], axis=0)

    def scores(c):
        off = pl.multiple_of(c * big, big)
        bias = m_ref[pl.ds(off, big), :].astype(F32)
        return off, _dot_nt(k_ref[pl.ds(off, big), :], qs) + jnp.concatenate([bias, bias], axis=1)

    def softmax_step(sm, m, l):
        m_new = jnp.maximum(m, jnp.max(sm, axis=0, keepdims=True))
        alpha = jnp.exp(m - m_new)
        p = jnp.exp(sm - m_new)
        return m_new, alpha, l * alpha + jnp.sum(p, axis=0, keepdims=True), p.astype(BF16)

    def pv(off, slot):
        out = _dot(vt_ref[off // ck], p_ref[slot, 0:ck, :])
        for u in range(1, big // ck):
            out = out + _dot(vt_ref[off // ck + u], p_ref[slot, u * ck:(u + 1) * ck, :])
        return out

    off0, sm = scores(0)
    m, alpha, l, pb = softmax_step(sm, jnp.full((1, 2 * nq), M_INIT, F32), jnp.zeros((1, 2 * nq), F32))
    p_ref[0] = pb
    off1, sm = scores(1)
    s_ref[1] = sm

    def body(c, carry):
        m, l, acc, alpha_prev, off_prev, off_cur = carry
        pending = pv(off_prev, (c - 1) % 2)
        m, alpha, l, pb = softmax_step(s_ref[c % 2], m, l)
        p_ref[c % 2] = pb
        off_next, sm_next = scores(c + 1)
        s_ref[(c + 1) % 2] = sm_next
        return m, l, acc * alpha_prev + pending, alpha, off_cur, off_next

    m, l, acc, alpha_prev, off_prev, off_cur = lax.fori_loop(
        1, n_it - 1, body, (m, l, jnp.zeros((HEAD_DIM, 2 * nq), F32), alpha, off0, off1))
    last = n_it - 1
    pending = pv(off_prev, (last - 1) % 2)
    _, alpha, l, pb = softmax_step(s_ref[last % 2], m, l)
    p_ref[last % 2] = pb
    acc = acc * alpha_prev + pending
    ot = (acc * alpha + pv(off_cur, last % 2)) / l
    o_ref[:, 0:HEAD_DIM] = ot[:, 0:nq].T.astype(BF16)
    o_ref[:, HEAD_DIM:2 * HEAD_DIM] = ot[:, nq:2 * nq].T.astype(BF16)


def _attend_prompt(qa, kb, vt3, mask, B, T, ck):
    nq = T // LANES
    M = B * T
    big = min(4 * ck, T // 2)
    return pl.pallas_call(
        functools.partial(_att_kernel, ck=ck, big=big),
        out_shape=jax.ShapeDtypeStruct((M, ATT_HEADS * HEAD_DIM), BF16),
        grid=(B, KV_HEADS, nq),
        in_specs=[pl.BlockSpec((LANES, 2 * HEAD_DIM), lambda b, g, i: (b * nq + i, g)),
                  pl.BlockSpec((T, HEAD_DIM), lambda b, g, i: (b, g)),
                  pl.BlockSpec((T // ck, HEAD_DIM, ck), lambda b, g, i: (b, g, 0)),
                  pl.BlockSpec((None, T, LANES), lambda b, g, i: (b, 0, i))],
        out_specs=pl.BlockSpec((LANES, 2 * HEAD_DIM), lambda b, g, i: (b * nq + i, g)),
        scratch_shapes=[pltpu.VMEM((2, big, 2 * LANES), BF16), pltpu.VMEM((2, big, 2 * LANES), F32)],
        compiler_params=_cp("parallel", "parallel", "arbitrary"),
        name="dsa_attend_prompt",
    )(qa, kb, vt3, mask)


def _score_step_kernel(pt_ref, iqt_ref, iw_ref, *refs, pps):
    ik_refs = refs[:pps]
    sc_ref = refs[pps]
    p = pl.program_id(1)
    iqt = iqt_ref[...]
    iw = iw_ref[...]
    mult = [jnp.broadcast_to(iqt[:, h:h + 1], (IDX_DIM, PAGE)) for h in range(IDX_HEADS)]
    for j in range(pps):
        pg = ik_refs[j][...]
        s = jnp.zeros((1, PAGE), F32)
        for h in range(IDX_HEADS):
            d = jnp.sum(pg * mult[h], axis=0, keepdims=True)
            s = s + jnp.maximum(d, 0.0) * iw[h:h + 1, :]
        sc_ref[pl.ds(p * pps + j, 1), :] = s


def _threshold_step_kernel(sc_ref, iq_ref, iw_ref, iko_ref, msk_ref, own_ref, *, K):
    key = _sortable(sc_ref[...])
    n_pages = key.shape[1]
    iw = iw_ref[...]
    d_own = jnp.sum(iq_ref[...] * iko_ref[...], axis=2, keepdims=True)
    s_own = jnp.sum(jnp.maximum(d_own, 0.0) * iw, axis=1, keepdims=True)
    key_own = _sortable(s_own)
    pos = lax.broadcasted_iota(I32, key.shape, 1) * PAGE + lax.broadcasted_iota(I32, key.shape, 2)
    pos_own = n_pages * PAGE

    def count(pred):
        c = jnp.sum(jnp.where(pred(key, pos), 1.0, 0.0), axis=1, keepdims=True)
        c = jnp.sum(c, axis=2, keepdims=True)
        return c + jnp.where(pred(key_own, pos_own), 1.0, 0.0)

    tau = jnp.full(key_own.shape, INT_MIN, I32)
    for bit in range(31, -1, -1):
        cand = jnp.zeros(key_own.shape, I32) if bit == 31 else tau | (1 << bit)
        tau = jnp.where(count(lambda kk, ps, cand=cand: kk >= cand) >= K, cand, tau)
    tau = jnp.maximum(tau, INT_MIN + 1)
    need = K - count(lambda kk, ps: kk > tau)
    jj = jnp.zeros(key_own.shape, I32)
    for bit in range((pos_own + 1).bit_length() - 1, -1, -1):
        cand = jj | (1 << bit)
        f = count(lambda kk, ps, cand=cand: jnp.where(kk == tau, ps, 2 ** 30) < cand)
        jj = jnp.where(f <= need, cand, jj)

    def selected(kk, ps):
        return jnp.where(kk > tau, 1.0, jnp.where(jnp.where(kk == tau, ps, 2 ** 30) < jj, 1.0, 0.0))

    msk_ref[...] = selected(key, pos)
    own_ref[...] = jnp.broadcast_to(selected(key_own, pos_own), own_ref.shape)


def _select_sample(page_table, iq8, iw8, ik_own, cache_idx_t, layer, pps):
    Bs, n_pages = page_table.shape
    K = min(TOPK_MAX, (n_pages * PAGE + 1) // 4)
    page_spec = lambda j: pl.BlockSpec((None, None, IDX_DIM, PAGE),
                                       lambda b, p, pt, j=j: (layer, pt[b, p * pps + j], 0, 0))
    grid_spec = pltpu.PrefetchScalarGridSpec(
        num_scalar_prefetch=1,
        grid=(Bs, n_pages // pps),
        in_specs=[pl.BlockSpec((None, IDX_DIM, IDX_HEADS), lambda b, p, pt: (b, 0, 0)),
                  pl.BlockSpec((None, IDX_HEADS, 1), lambda b, p, pt: (b, 0, 0))]
                 + [page_spec(j) for j in range(pps)],
        out_specs=pl.BlockSpec((None, n_pages, PAGE), lambda b, p, pt: (b, 0, 0)),
    )
    scores = pl.pallas_call(
        functools.partial(_score_step_kernel, pps=pps),
        out_shape=jax.ShapeDtypeStruct((Bs, n_pages, PAGE), F32),
        grid_spec=grid_spec,
        compiler_params=_cp("parallel", "arbitrary"),
        name="dsa_score_sample",
    )(page_table, jnp.swapaxes(iq8, 1, 2), iw8, *([cache_idx_t] * pps))
    whole = lambda shape: pl.BlockSpec(shape, lambda i: (0,) * len(shape))
    return pl.pallas_call(
        functools.partial(_threshold_step_kernel, K=K),
        out_shape=[jax.ShapeDtypeStruct((Bs, n_pages, PAGE), F32), jax.ShapeDtypeStruct((Bs, 1, LANES), F32)],
        grid=(1,),
        in_specs=[whole((Bs, n_pages, PAGE)), whole((Bs, IDX_HEADS, IDX_DIM)), whole((Bs, IDX_HEADS, 1)),
                  whole((Bs, 1, IDX_DIM))],
        out_specs=[whole((Bs, n_pages, PAGE)), whole((Bs, 1, LANES))],
        compiler_params=_cp("arbitrary"),
        name="dsa_threshold_sample",
    )(scores, iq8, iw8, ik_own)


def _att_step_kernel(pt_ref, q_ref, msk_ref, own_ref, ko_ref, vo_ref, *refs, pp):
    k_refs, v_refs = refs[:pp], refs[pp:2 * pp]
    o_ref, m_sc, l_sc, acc_sc = refs[2 * pp:]
    p = pl.program_id(1)
    rows = PAGE * KV_HEADS
    grp = ATT_HEADS // KV_HEADS

    @pl.when(p == 0)
    def _():
        m_sc[...] = jnp.full(m_sc.shape, NEG, F32)
        l_sc[...] = jnp.zeros(l_sc.shape, F32)
        acc_sc[...] = jnp.zeros(acc_sc.shape, F32)

    q8 = q_ref[...]
    own_kv = (lax.broadcasted_iota(I32, (ATT_HEADS, rows), 1) % KV_HEADS
              == lax.broadcasted_iota(I32, (ATT_HEADS, rows), 0) // grp)
    expand = jnp.where(lax.broadcasted_iota(I32, (PAGE, rows), 1) // KV_HEADS
                       == lax.broadcasted_iota(I32, (PAGE, rows), 0), 1.0, 0.0).astype(BF16)
    sms = []
    for j in range(pp):
        s = _dot_nt(q8, k_refs[j][...].astype(BF16))
        mrow = jnp.broadcast_to(msk_ref[pl.ds(p * pp + j, 1), :], (ATT_HEADS, PAGE)).astype(BF16)
        picked = _dot(mrow, expand)
        sms.append(jnp.where(jnp.where(own_kv, picked, 0.0) > 0.0, s, NEG))
    m = m_sc[...]
    m_new = m
    for sm in sms:
        m_new = jnp.maximum(m_new, jnp.max(sm, axis=1, keepdims=True))
    alpha = jnp.exp(m - m_new)
    l = l_sc[...] * alpha
    acc = acc_sc[...] * alpha
    for j in range(pp):
        pr = jnp.where(sms[j] > 0.5 * NEG, jnp.exp(sms[j] - m_new), 0.0)
        l = l + jnp.sum(pr, axis=1, keepdims=True)
        acc = acc + _dot(pr.astype(BF16), v_refs[j][...].astype(BF16))
    m_sc[...], l_sc[...], acc_sc[...] = m_new, l, acc

    @pl.when(p == pl.num_programs(1) - 1)
    def _():
        ko = ko_ref[...].astype(BF16).astype(F32)
        s_own = jnp.sum(q8.astype(F32) * ko, axis=1, keepdims=True)
        sel_own = own_ref[:, 0:1] > 0.0
        sm = jnp.where(sel_own, s_own, NEG)
        m_fin = jnp.maximum(m_new, sm)
        a_fin = jnp.exp(m_new - m_fin)
        pr = jnp.where(sel_own, jnp.exp(sm - m_fin), 0.0)
        o = (acc * a_fin + pr * vo_ref[...]) / (l * a_fin + pr)
        o_ref[...] = o.astype(BF16)


def _attend_sample(page_table, q8, msk, own, k_own8, v_own8, cache_k, cache_v, layer, pp):
    Bs, n_pages = page_table.shape
    rows = PAGE * KV_HEADS
    page_spec = lambda j: pl.BlockSpec((None, None, rows, HEAD_DIM),
                                       lambda b, p, pt, j=j: (layer, pt[b, p * pp + j], 0, 0))
    per_b = lambda r, w: pl.BlockSpec((None, r, w), lambda b, p, pt: (b, 0, 0))
    grid_spec = pltpu.PrefetchScalarGridSpec(
        num_scalar_prefetch=1,
        grid=(Bs, n_pages // pp),
        in_specs=[per_b(ATT_HEADS, HEAD_DIM), per_b(n_pages, PAGE), per_b(1, LANES),
                  per_b(ATT_HEADS, HEAD_DIM), per_b(ATT_HEADS, HEAD_DIM)]
                 + [page_spec(j) for j in range(pp)] + [page_spec(j) for j in range(pp)],
        out_specs=per_b(ATT_HEADS, HEAD_DIM),
        scratch_shapes=[pltpu.VMEM((ATT_HEADS, 1), F32), pltpu.VMEM((ATT_HEADS, 1), F32),
                        pltpu.VMEM((ATT_HEADS, HEAD_DIM), F32)],
    )
    return pl.pallas_call(
        functools.partial(_att_step_kernel, pp=pp),
        out_shape=jax.ShapeDtypeStruct((Bs, ATT_HEADS, HEAD_DIM), BF16),
        grid_spec=grid_spec,
        compiler_params=_cp("parallel", "arbitrary"),
        name="dsa_attend_sample",
    )(page_table, q8, msk, own, k_own8, v_own8, *([cache_k] * pp), *([cache_v] * pp))


def _mixout_kernel(x_ref, ro_ref, ca_ref, ao_ref, g0_ref, g1_ref, g2_ref, gm_ref,
                   wr_ref, wc_ref, wa_ref, wo_ref, o_ref):
    merged = (jax.nn.sigmoid(g0_ref[...]) * _dot(ro_ref[...], wr_ref[...])
              + jax.nn.sigmoid(g1_ref[...]) * _dot(ca_ref[...], wc_ref[...])
              + jax.nn.sigmoid(g2_ref[...]) * _dot(ao_ref[...], wa_ref[...]))
    o_ref[...] = x_ref[...] + gm_ref[0] * _dot(merged.astype(BF16), wo_ref[...])


def _mixout(x, ro, ca, ao, z, gate, w_r, w_c, w_a, w_o, tm, rpg):
    M, D = x.shape
    row = pl.BlockSpec((tm, D), lambda i: (i, 0))
    glc = lambda k: pl.BlockSpec((tm, D), lambda i, o=Z_GL // D + k: (i, o))
    wsp = pl.BlockSpec((D, D), lambda i: (0, 0))
    return pl.pallas_call(
        _mixout_kernel,
        out_shape=jax.ShapeDtypeStruct((M, D), F32),
        grid=(M // tm,),
        in_specs=[row, row, row, row, glc(0), glc(1), glc(2), _mod_spec(gate, tm, rpg), wsp, wsp, wsp, wsp],
        out_specs=row,
        compiler_params=_cp("parallel"),
        name="mixer_out",
    )(x, ro, ca, ao, z, z, z, gate, w_r, w_c, w_a, w_o)


def _rope_tables(pos, rot_dim, theta, period):
    half = rot_dim // 2
    inv = 1.0 / (theta ** (jnp.arange(half, dtype=F32) / half))
    ang = pos.astype(F32)[:, None] * inv[None, :]
    lane = np.arange(LANES) % period
    idx = lane % half
    cos = jnp.where(lane < rot_dim, jnp.cos(ang)[:, idx], 1.0)
    sin = jnp.sin(ang)[:, idx]
    s_lo = jnp.where(lane < half, -sin, 0.0)
    s_hi = jnp.where((lane >= half) & (lane < rot_dim), sin, 0.0)
    return cos, s_lo, s_hi


def _tables(pos):
    ca = _rope_tables(pos, ROT_DIM, ROPE_THETA, HEAD_DIM)
    ci = _rope_tables(pos, IDX_ROT, ROPE_THETA, IDX_DIM)
    cr, r_lo, r_hi = _rope_tables(pos, RET_DK, RET_THETA, RET_DK)
    return ca, ci, (cr, r_lo + r_hi)


def _mods(mod, rows):
    parts = jnp.split(mod, 9, axis=-1)
    if rows == 1:
        return [p[:, None, :] for p in parts]
    return [p[None] for p in parts]


def _cat_w_in(w):
    pad = Z_END - Z_IX - (IDX_DIM + IDX_HEADS)
    return jnp.concatenate([w[:, :6144], w[:, 7752:10824], w[:, 6144:7680], w[:, 7680:7752],
                            jnp.zeros((w.shape[0], pad), w.dtype)], axis=1).astype(BF16)


def kernel(x_prompt, x_sample, cache_k, cache_v, cache_idx_k, state_ret, state_conv, page_table, c_prompt, c_sample, w_ada, b_ada, norm_g, w_ff1_up, w_ff1_down, w_ff2_up, w_ff2_down, w_in, conv_w, conv_b, conv_g, q_norm_g, k_norm_g, idx_k_norm_g, w_ret_o, w_conv_o, w_att_o, w_o):
    B, T, D = x_prompt.shape
    Bs, Ts, _ = x_sample.shape
    assert Ts == 1 and T % 512 == 0 and D == 1024
    depth = w_ada.shape[0]
    n_pages = page_table.shape[1]
    past = n_pages * PAGE
    tm_p = 512
    ck = 256
    pps = 16 if n_pages % 16 == 0 else 4
    pp = 32 if n_pages % 32 == 0 else 4
    grp = ATT_HEADS // KV_HEADS

    tabs_p = _tables(jnp.arange(T, dtype=jnp.int32))
    tabs_s = _tables(jnp.full((Bs,), past, dtype=jnp.int32))
    lg = jnp.broadcast_to(jnp.log(1.0 - 2.0 ** (-5.0 - jnp.arange(RET_HEADS, dtype=F32)))[:, None],
                          (RET_HEADS, LANES))
    c_all = jnp.concatenate([c_prompt, c_sample], axis=0)
    ck4 = cache_k.reshape(cache_k.shape[0], cache_k.shape[1], PAGE * KV_HEADS, HEAD_DIM)
    cv4 = cache_v.reshape(cache_v.shape[0], cache_v.shape[1], PAGE * KV_HEADS, HEAD_DIM)
    cache_idx_t = jnp.swapaxes(cache_idx_k, 2, 3)
    conv_b3 = conv_b.reshape(depth, 1, D)
    conv_g3 = conv_g.reshape(depth, 1, D)

    xp = x_prompt.reshape(B * T, D)
    xs = x_sample.reshape(Bs, D)
    outs_p, outs_s = [], []
    for l in range(depth):
        mod = _ada_mod(c_all, w_ada, b_ada, l)
        mp = _mods(mod[:B], 1)
        msm = _mods(mod[B:], Bs)
        wu1, wd1 = w_ff1_up[l].astype(BF16), w_ff1_down[l].astype(BF16)
        wu2, wd2 = w_ff2_up[l].astype(BF16), w_ff2_down[l].astype(BF16)
        w_cat = _cat_w_in(w_in[l])
        w_r, w_c = w_ret_o[l].astype(BF16), w_conv_o[l].astype(BF16)
        w_a, w_oo = w_att_o[l].astype(BF16), w_o[l].astype(BF16)
        ng = norm_g[l]
        qg, kg = q_norm_g[l][None, :], k_norm_g[l][None, :]
        ikg = jnp.concatenate([idx_k_norm_g[l], jnp.zeros((LANES - IDX_DIM,), F32)])[None, :]

        xp = _ffn(xp, mp[0], mp[1], mp[2], ng[0:1], wu1, wd1, tm_p, T)
        z = _inproj(xp, mp[3], mp[4], ng[1:2], w_cat, min(1024, T), T)
        (qa, k_p, kb, v_p, _iqr, _ixo, idxk_p, vt3, iqth, iqtl, ik3, iwt) = _prep(
            z, tabs_p[0], tabs_p[1], qg, kg, ikg, tm_p, True, ck)
        ro, ret_p = _retention_prompt(z.reshape(B, T, Z_END), tabs_p[2][0], tabs_p[2][1], lg)
        ca, conv_p = _conv_prompt(z, B, T, conv_w, conv_b3, conv_g3, l, tm_p)
        mask = _select_prompt(ik3, iqth, iqtl, iwt, B, T, 2 * ck)
        ao = _attend_prompt(qa, kb, vt3, mask, B, T, ck)
        xp = _mixout(xp, ro.reshape(B * T, D), ca, ao, z, mp[5], w_r, w_c, w_a, w_oo, tm_p, T)
        xp = _ffn(xp, mp[6], mp[7], mp[8], ng[2:3], wu2, wd2, tm_p, T)
        outs_p.append((k_p.reshape(B, T, KV_HEADS, HEAD_DIM), v_p.reshape(B, T, KV_HEADS, HEAD_DIM),
                       idxk_p.reshape(B, T, IDX_DIM), ret_p, conv_p))

        xs = _ffn(xs, msm[0], msm[1], msm[2], ng[0:1], wu1, wd1, Bs, Bs)
        zs = _inproj(xs, msm[3], msm[4], ng[1:2], w_cat, Bs, Bs)
        qa_s, k_s, _kb, v_s, iqr_s, ixo_s, idxk_s = _prep(
            zs, tabs_s[0], tabs_s[1], qg, kg, ikg, Bs, False, ck)
        zs3 = zs.reshape(Bs, 1, Z_END)
        ro_s, ret_s = _retention_sample(zs3, state_ret, l, tabs_s[2][0][0:1], tabs_s[2][1][0:1], lg)
        ca_s, conv_s = _conv_sample(zs3, state_conv, conv_w, conv_b3, conv_g3, l)
        iq8 = iqr_s.reshape(Bs, IDX_HEADS, IDX_DIM)
        iw8 = ixo_s[:, IDX_DIM:IDX_DIM + IDX_HEADS].reshape(Bs, IDX_HEADS, 1)
        msk, own = _select_sample(page_table, iq8, iw8, idxk_s.reshape(Bs, 1, IDX_DIM), cache_idx_t, l, pps)
        k_own8 = jnp.repeat(k_s.reshape(Bs, KV_HEADS, HEAD_DIM), grp, axis=1)
        v_own8 = jnp.repeat(v_s.reshape(Bs, KV_HEADS, HEAD_DIM), grp, axis=1)
        ao_s = _attend_sample(page_table, qa_s.reshape(Bs, ATT_HEADS, HEAD_DIM), msk, own,
                              k_own8, v_own8, ck4, cv4, l, pp)
        xs = _mixout(xs, ro_s.reshape(Bs, D), ca_s.reshape(Bs, D), ao_s.reshape(Bs, D), zs, msm[5],
                     w_r, w_c, w_a, w_oo, Bs, Bs)
        xs = _ffn(xs, msm[6], msm[7], msm[8], ng[2:3], wu2, wd2, Bs, Bs)
        outs_s.append((k_s.reshape(Bs, 1, KV_HEADS, HEAD_DIM), v_s.reshape(Bs, 1, KV_HEADS, HEAD_DIM),
                       idxk_s.reshape(Bs, 1, IDX_DIM), ret_s, conv_s))

    stack = lambda outs, j: jnp.stack([o[j] for o in outs])
    return (xp.reshape(B, T, D), xs.reshape(Bs, 1, D),
            stack(outs_p, 0), stack(outs_p, 1), stack(outs_p, 2), stack(outs_p, 3), stack(outs_p, 4),
            stack(outs_s, 0), stack(outs_s, 1), stack(outs_s, 2), stack(outs_s, 3), stack(outs_s, 4))
```
